```python
import jax, jax.numpy as jnp
from jax import lax
import numpy as np

D_MODEL = 1024
BATCH = 1
SEQ = 16384
DEPTH = 4

ATT_HEAD_DIM = 64
ATT_WIDTH = D_MODEL // 2
ATT_HEADS = ATT_WIDTH // ATT_HEAD_DIM
Q_BLOCK = 128
LRU_WIDTH = D_MODEL
LRU_BLOCK_DIM = 64
LRU_BLOCKS = LRU_WIDTH // LRU_BLOCK_DIM
LRU_CONV = 4
LRU_C = 8.0
RWKV_HEAD_DIM = 64
RWKV_WIDTH = D_MODEL // 2
RWKV_HEADS = RWKV_WIDTH // RWKV_HEAD_DIM
DECAY_LORA = 64
AAA_LORA = 64
GATE_LORA = 128
RWKV_GN_EPS = 64e-5
D_FF = ((8 * D_MODEL // 3 + 127) // 128) * 128
N_BRANCH = 3
N_SUB = 3
N_ADA = 3 * N_SUB
LN_EPS = 1e-5
DEEPNORM_ALPHA = (2 * DEPTH) ** 0.25
DEEPNORM_BETA = (8 * DEPTH) ** -0.25
ATT_COLS = 3 * ATT_WIDTH + ATT_HEADS
LRU_COLS = 2 * LRU_WIDTH
RWKV_COLS = 3 * RWKV_WIDTH + DECAY_LORA + AAA_LORA + GATE_LORA
GATE_COLS = N_BRANCH * D_MODEL
D_IN = ATT_COLS + LRU_COLS + RWKV_COLS + GATE_COLS

kernel_name = "hybrid_fox_rglru_rwkv7_macaron_deepnorm"


def _split(z, sizes):
    return jnp.split(z, np.cumsum(sizes)[:-1].tolist(), axis=-1)


def _ln_stats(x):
    xf = x.astype(jnp.float32)
    mu = jnp.mean(xf, axis=-1, keepdims=True)
    var = jnp.mean(jnp.square(xf - mu), axis=-1, keepdims=True)
    return (xf - mu) * lax.rsqrt(var + LN_EPS)


def _post_norm(x, g, b):
    return (_ln_stats(x) * g + b).astype(x.dtype)


def _modulate(x, shift, scale):
    return (_ln_stats(x) * (1 + scale[:, None, :]) + shift[:, None, :]).astype(x.dtype)


def _prev_token(z):
    return jnp.pad(z, ((0, 0), (1, 0), (0, 0)))[:, :-1]


def _swiglu(h, w_up, w_down):
    u, gte = _split(h @ w_up, [D_FF, D_FF])
    return (jax.nn.silu(u) * gte) @ w_down


def _fox_attention(q, k, v, f_logit, f_bias):
    B, S, _ = q.shape
    H, dh = ATT_HEADS, ATT_HEAD_DIM
    heads = lambda t: t.reshape(B, S, H, dh).transpose(0, 2, 1, 3)
    q, k, v = heads(q), heads(k), heads(v)
    log_f = jax.nn.log_sigmoid((f_logit + f_bias).astype(jnp.float32))
    F = jnp.cumsum(log_f, axis=1).transpose(0, 2, 1)
    nb = S // Q_BLOCK
    qb = q.reshape(B, H, nb, Q_BLOCK, dh).transpose(2, 0, 1, 3, 4)
    Fb = F.reshape(B, H, nb, Q_BLOCK).transpose(2, 0, 1, 3)
    qpos = jnp.arange(S).reshape(nb, Q_BLOCK)
    kpos = jnp.arange(S)
    scale = dh ** -0.5

    def block(args):
        qi, Fi, pi = args
        s = jnp.einsum('bhqd,bhkd->bhqk', qi, k).astype(jnp.float32) * scale
        s = s + (Fi[..., :, None] - F[..., None, :])
        s = jnp.where(kpos[None, :] <= pi[:, None], s, -jnp.inf)
        p = jax.nn.softmax(s, axis=-1).astype(v.dtype)
        return jnp.einsum('bhqk,bhkd->bhqd', p, v)

    o = lax.map(block, (qb, Fb, qpos))
    return o.transpose(1, 0, 3, 2, 4).reshape(B, S, H * dh)


def _causal_dwconv(x, w, b):
    S = x.shape[1]
    K = w.shape[0]
    xp = jnp.pad(x, ((0, 0), (K - 1, 0), (0, 0)))
    y = xp[:, 0:S] * w[0]
    for j in range(1, K):
        y = y + xp[:, j:j + S] * w[j]
    return y + b


def _lin_combine(left, right):
    a_l, b_l = left
    a_r, b_r = right
    return a_l * a_r, a_r * b_l + b_r


def _rg_lru_branch(xb, yb, conv_w, conv_b, ga_w, ga_b, gx_w, gx_b, lam):
    B, S, _ = xb.shape
    xc = _causal_dwconv(xb, conv_w, conv_b)
    xh = xc.reshape(B, S, LRU_BLOCKS, LRU_BLOCK_DIM)
    r = jax.nn.sigmoid(jnp.einsum('bsni,nij->bsnj', xh, ga_w).reshape(B, S, LRU_WIDTH) + ga_b)
    i = jax.nn.sigmoid(jnp.einsum('bsni,nij->bsnj', xh, gx_w).reshape(B, S, LRU_WIDTH) + gx_b)
    log_a = -LRU_C * r.astype(jnp.float32) * jax.nn.softplus(-lam.astype(jnp.float32))
    a = jnp.exp(log_a)
    u = jnp.sqrt(-jnp.expm1(2 * log_a)) * (i * xc).astype(jnp.float32)
    _, h = lax.associative_scan(_lin_combine, (a, u), axis=1)
    return h.astype(xb.dtype) * jax.nn.gelu(yb)


def _rwkv7_branch(z, mu, w0, w2, a0, a2, g2, k_k, k_a, r_k, gn_w, gn_b):
    B, S, _ = z.shape
    H, N = RWKV_HEADS, RWKV_HEAD_DIM
    f32 = jnp.float32
    z = z + (_prev_token(z) - z) * mu
    r, k, v, wl, al, gl = _split(z, [RWKV_WIDTH] * 3 + [DECAY_LORA, AAA_LORA, GATE_LORA])
    w = -jax.nn.softplus(-(w0 + jnp.tanh(wl) @ w2).astype(f32)) - 0.5
    decay = jnp.exp(-jnp.exp(w))
    a = jax.nn.sigmoid((a0 + al @ a2).astype(f32))
    g = jax.nn.sigmoid(gl) @ g2
    heads = lambda t: t.reshape(B, S, H, N)
    kk = heads((k * k_k).astype(f32))
    kk = kk / jnp.maximum(jnp.linalg.norm(kk, axis=-1, keepdims=True), 1e-12)
    kf = k.astype(f32) * (1 + (a - 1) * k_a.astype(f32))
    rh, kh, vh, ah = heads(r.astype(f32)), heads(kf), heads(v.astype(f32)), heads(a)
    xs = tuple(t.transpose(1, 0, 2, 3) for t in (rh, heads(decay), kh, vh, -kk, kk * ah))

    def step(state, inp):
        r_t, w_t, k_t, v_t, a_t, b_t = inp
        sa = jnp.einsum('bhvk,bhk->bhv', state, a_t)
        state = (state * w_t[:, :, None, :] + sa[..., None] * b_t[:, :, None, :]
                 + v_t[..., None] * k_t[:, :, None, :])
        return state, jnp.einsum('bhvk,bhk->bhv', state, r_t)

    state0 = jnp.zeros((B, H, N, N), f32)
    _, ys = lax.scan(step, state0, xs)
    y = ys.transpose(1, 0, 2, 3)
    m = jnp.mean(y, axis=-1, keepdims=True)
    var = jnp.mean(jnp.square(y - m), axis=-1, keepdims=True)
    y = (y - m) * lax.rsqrt(var + RWKV_GN_EPS) * gn_w.reshape(H, N) + gn_b.reshape(H, N)
    bonus = jnp.sum(rh * kh * r_k, axis=-1, keepdims=True) * vh
    return (y + bonus).reshape(B, S, RWKV_WIDTH).astype(z.dtype) * g


def _mixer(h, w_in, f_bias, conv_w, conv_b, ga_w, ga_b, gx_w, gx_b, lam,
           mu, w0, w2, a0, a2, g2, k_k, k_a, r_k, gn_w, gn_b, wpa, wpb, wpc, w_out):
    z = h @ w_in
    za, zb, zc, zg = _split(z, [ATT_COLS, LRU_COLS, RWKV_COLS, GATE_COLS])
    q, k, v, fl = _split(za, [ATT_WIDTH] * 3 + [ATT_HEADS])
    o_a = _fox_attention(q, k, v, fl, f_bias) @ wpa
    xb, yb = _split(zb, [LRU_WIDTH, LRU_WIDTH])
    o_b = _rg_lru_branch(xb, yb, conv_w, conv_b, ga_w, ga_b, gx_w, gx_b, lam) @ wpb
    o_c = _rwkv7_branch(zc, mu, w0, w2, a0, a2, g2, k_k, k_a, r_k, gn_w, gn_b) @ wpc
    g_a, g_b, g_c = _split(jax.nn.sigmoid(zg), [D_MODEL] * N_BRANCH)
    return (g_a * o_a + g_b * o_b + g_c * o_c) @ w_out


def setup_inputs(seed: int = 0) -> dict:
    key = jax.random.key(seed)
    ks = iter(jax.random.split(key, 40))
    f32 = jnp.float32
    nrm = lambda shape, s: jax.random.normal(next(ks), shape, f32) * s
    uni = lambda shape, lo, hi: jax.random.uniform(next(ks), shape, f32, minval=lo, maxval=hi)
    L, D = DEPTH, D_MODEL
    u = uni((L, LRU_WIDTH), 0.9, 0.999)
    s = u ** (1.0 / LRU_C)
    lam = jnp.log(s) - jnp.log1p(-s)
    return {
        "x": nrm((BATCH, SEQ, D), 1.0),
        "c": nrm((BATCH, D), 1.0),
        "ada_w": nrm((L, D, N_ADA * D), D ** -0.5),
        "ada_b": nrm((L, N_ADA * D), 0.01),
        "ln_g": 1.0 + nrm((L, N_SUB, D), 0.02),
        "ln_b": nrm((L, N_SUB, D), 0.02),
        "ffn_up": nrm((L, 2, D, 2 * D_FF), D ** -0.5),
        "ffn_down": nrm((L, 2, D_FF, D), D_FF ** -0.5 * DEEPNORM_BETA),
        "w_in": nrm((L, D, D_IN), D ** -0.5),
        "fox_f_bias": uni((L, ATT_HEADS), 1.0, 6.0),
        "lru_conv_w": nrm((L, LRU_CONV, LRU_WIDTH), LRU_CONV ** -0.5),
        "lru_conv_b": nrm((L, LRU_WIDTH), 0.01),
        "lru_ga_w": nrm((L, LRU_BLOCKS, LRU_BLOCK_DIM, LRU_BLOCK_DIM), LRU_BLOCK_DIM ** -0.5),
        "lru_ga_b": nrm((L, LRU_WIDTH), 0.01),
        "lru_gx_w": nrm((L, LRU_BLOCKS, LRU_BLOCK_DIM, LRU_BLOCK_DIM), LRU_BLOCK_DIM ** -0.5),
        "lru_gx_b": nrm((L, LRU_WIDTH), 0.01),
        "lru_lambda": lam,
        "rwkv_mu": uni((L, RWKV_COLS), 0.0, 1.0),
        "rwkv_w0": uni((L, RWKV_WIDTH), -6.0, -1.0),
        "rwkv_w2": nrm((L, DECAY_LORA, RWKV_WIDTH), 0.1 * DECAY_LORA ** -0.5),
        "rwkv_a0": nrm((L, RWKV_WIDTH), 0.1),
        "rwkv_a2": nrm((L, AAA_LORA, RWKV_WIDTH), 0.1 * AAA_LORA ** -0.5),
        "rwkv_g2": nrm((L, GATE_LORA, RWKV_WIDTH), GATE_LORA ** -0.5),
        "rwkv_k_k": 0.85 + nrm((L, RWKV_WIDTH), 0.05),
        "rwkv_k_a": 1.0 + nrm((L, RWKV_WIDTH), 0.05),
        "rwkv_r_k": nrm((L, RWKV_HEADS, RWKV_HEAD_DIM), 0.1),
        "rwkv_gn_w": 1.0 + nrm((L, RWKV_WIDTH), 0.02),
        "rwkv_gn_b": nrm((L, RWKV_WIDTH), 0.02),
        "w_proj_a": nrm((L, ATT_WIDTH, D), ATT_WIDTH ** -0.5 * DEEPNORM_BETA),
        "w_proj_b": nrm((L, LRU_WIDTH, D), LRU_WIDTH ** -0.5 * DEEPNORM_BETA),
        "w_proj_c": nrm((L, RWKV_WIDTH, D), RWKV_WIDTH ** -0.5 * DEEPNORM_BETA),
        "w_out": nrm((L, D, D), D ** -0.5 * DEEPNORM_BETA),
    }


def reference(x, c, ada_w, ada_b, ln_g, ln_b, ffn_up, ffn_down, w_in, fox_f_bias,
              lru_conv_w, lru_conv_b, lru_ga_w, lru_ga_b, lru_gx_w, lru_gx_b, lru_lambda,
              rwkv_mu, rwkv_w0, rwkv_w2, rwkv_a0, rwkv_a2, rwkv_g2, rwkv_k_k, rwkv_k_a, rwkv_r_k,
              rwkv_gn_w, rwkv_gn_b, w_proj_a, w_proj_b, w_proj_c, w_out):
    c_act = jax.nn.silu(c)
    for l in range(DEPTH):
        ada = c_act @ ada_w[l] + ada_b[l]
        sh1, sc1, g1, sh2, sc2, g2, sh3, sc3, g3 = jnp.split(ada, N_ADA, axis=-1)
        h = _modulate(x, sh1, sc1)
        y = 0.5 * g1[:, None, :] * _swiglu(h, ffn_up[l, 0], ffn_down[l, 0])
        x = _post_norm(DEEPNORM_ALPHA * x + y, ln_g[l, 0], ln_b[l, 0])
        h = _modulate(x, sh2, sc2)
        y = g2[:, None, :] * _mixer(
            h, w_in[l], fox_f_bias[l],
            lru_conv_w[l], lru_conv_b[l], lru_ga_w[l], lru_ga_b[l], lru_gx_w[l], lru_gx_b[l], lru_lambda[l],
            rwkv_mu[l], rwkv_w0[l], rwkv_w2[l], rwkv_a0[l], rwkv_a2[l], rwkv_g2[l], rwkv_k_k[l], rwkv_k_a[l],
            rwkv_r_k[l], rwkv_gn_w[l], rwkv_gn_b[l],
            w_proj_a[l], w_proj_b[l], w_proj_c[l], w_out[l])
        x = _post_norm(DEEPNORM_ALPHA * x + y, ln_g[l, 1], ln_b[l, 1])
        h = _modulate(x, sh3, sc3)
        y = 0.5 * g3[:, None, :] * _swiglu(h, ffn_up[l, 1], ffn_down[l, 1])
        x = _post_norm(DEEPNORM_ALPHA * x + y, ln_g[l, 2], ln_b[l, 2])
    return x
```

```python
import functools

import jax
import jax.numpy as jnp
from jax import lax
from jax.experimental import pallas as pl
from jax.experimental.pallas import tpu as pltpu

F32 = jnp.float32
BF16 = jnp.bfloat16
HI = lax.Precision.HIGHEST

LANES = 128
HEAD_DIM = 64
N_HEADS = 8
ATT_WIDTH = 512
LRU_WIDTH = 1024
LRU_BLOCK = 64
LRU_CONV = 4
LRU_C = 8.0
RWKV_WIDTH = 512
LORA_PAD = 128
RWKV_COLS_PAD = 2048
RWKV_GN_EPS = 64e-5
LN_EPS = 1e-5
NEG_BIG = -1e30
VMEM_LIMIT = 56 * 1024 * 1024

RWKV_CHUNK = 64


def _dot(a, b, prec=None):
    return jnp.dot(a, b, preferred_element_type=F32, precision=prec)


def _dot_nt(a, b, prec=None):
    return lax.dot_general(a, b, (((1,), (1,)), ((), ())), preferred_element_type=F32, precision=prec)


def _ln(x):
    mu = jnp.mean(x, axis=-1, keepdims=True)
    xc = x - mu
    var = jnp.mean(xc * xc, axis=-1, keepdims=True)
    return xc * lax.rsqrt(var + LN_EPS)


def _softplus(x):
    return jnp.maximum(x, 0.0) + jnp.log1p(jnp.exp(-jnp.abs(x)))


def _sigmoid(x):
    return 1.0 / (1.0 + jnp.exp(-x))


def _params(sem):
    return pltpu.CompilerParams(dimension_semantics=sem, vmem_limit_bytes=VMEM_LIMIT)


def _pack_rows(rows, width):
    rows = [r.reshape(1, width).astype(F32) for r in rows]
    pad = jnp.zeros((8 - len(rows), width), F32)
    return jnp.concatenate(rows + [pad], axis=0)


def _ada_kernel(c_ref, w_ref, b_ref, o_ref):
    c = c_ref[...]
    ca = c * _sigmoid(c)
    o_ref[0] = _dot(ca, w_ref[0], HI) + b_ref[0]


def _ada_all(c, ada_w, ada_b):
    n_layers, d, n = ada_w.shape
    tn = 1152
    c8 = jnp.broadcast_to(c.astype(F32), (8, d))
    out = pl.pallas_call(
        _ada_kernel,
        grid=(n_layers, n // tn),
        in_specs=[
            pl.BlockSpec((8, d), lambda l, j: (0, 0)),
            pl.BlockSpec((1, d, tn), lambda l, j: (l, 0, j)),
            pl.BlockSpec((1, 1, tn), lambda l, j: (l, 0, j)),
        ],
        out_specs=pl.BlockSpec((1, 8, tn), lambda l, j: (l, 0, j)),
        out_shape=jax.ShapeDtypeStruct((n_layers, 8, n), F32),
        compiler_params=_params(("arbitrary", "arbitrary")),
    )(c8, ada_w, ada_b.reshape(n_layers, 1, n))
    return out[:, 0, :]


def _ffn_kernel(x_ref, mod_ref, lnp_ref, wu_ref, wg_ref, wd_ref, o_ref, h_scr, acc_scr, *, alpha):
    j = pl.program_id(1)

    @pl.when(j == 0)
    def _():
        h = _ln(x_ref[...]) * (1.0 + mod_ref[1:2, :]) + mod_ref[0:1, :]
        h_scr[...] = h.astype(BF16)
        acc_scr[...] = jnp.zeros_like(acc_scr)

    h = h_scr[...]
    u = _dot(h, wu_ref[...])
    g = _dot(h, wg_ref[...])
    act = (u * _sigmoid(u) * g).astype(BF16)
    acc_scr[...] += _dot(act, wd_ref[...])

    @pl.when(j == pl.num_programs(1) - 1)
    def _():
        y = (0.5 * mod_ref[2:3, :]) * acc_scr[...]
        z = alpha * x_ref[...] + y
        o_ref[...] = _ln(z) * lnp_ref[0:1, :] + lnp_ref[1:2, :]


def _ffn(x, mod, lnp, w_up, w_down, alpha):
    s, d = x.shape
    d_ff = w_down.shape[0]
    tm = min(1024, s)
    tf = 256
    nf = d_ff // tf
    return pl.pallas_call(
        functools.partial(_ffn_kernel, alpha=alpha),
        grid=(s // tm, nf),
        in_specs=[
            pl.BlockSpec((tm, d), lambda i, j: (i, 0)),
            pl.BlockSpec((8, d), lambda i, j: (0, 0)),
            pl.BlockSpec((8, d), lambda i, j: (0, 0)),
            pl.BlockSpec((d, tf), lambda i, j: (0, j)),
            pl.BlockSpec((d, tf), lambda i, j: (0, j + nf)),
            pl.BlockSpec((tf, d), lambda i, j: (j, 0)),
        ],
        out_specs=pl.BlockSpec((tm, d), lambda i, j: (i, 0)),
        out_shape=jax.ShapeDtypeStruct((s, d), F32),
        scratch_shapes=[pltpu.VMEM((tm, d), BF16), pltpu.VMEM((tm, d), F32)],
        compiler_params=_params(("parallel", "arbitrary")),
    )(x, mod, lnp, w_up, w_up, w_down)


def _ln_matmul_kernel(x_ref, mod_ref, w_ref, o_ref, h_scr, *, act):
    @pl.when(pl.program_id(1) == 0)
    def _():
        h = _ln(x_ref[...]) * (1.0 + mod_ref[1:2, :]) + mod_ref[0:1, :]
        h_scr[...] = h.astype(BF16)

    z = _dot(h_scr[...], w_ref[...])
    if act == "sigmoid":
        z = _sigmoid(z)
    o_ref[...] = z.astype(o_ref.dtype)


def _ln_matmul(x, mod, w, out_dtype, act=None):
    s, d = x.shape
    n = w.shape[1]
    tm = min(512, s)
    tn = next(t for t in (512, 256, 128) if n % t == 0)
    return pl.pallas_call(
        functools.partial(_ln_matmul_kernel, act=act),
        grid=(s // tm, n // tn),
        in_specs=[
            pl.BlockSpec((tm, d), lambda i, j: (i, 0)),
            pl.BlockSpec((8, d), lambda i, j: (0, 0)),
            pl.BlockSpec((d, tn), lambda i, j: (0, j)),
        ],
        out_specs=pl.BlockSpec((tm, tn), lambda i, j: (i, j)),
        out_shape=jax.ShapeDtypeStruct((s, n), out_dtype),
        scratch_shapes=[pltpu.VMEM((tm, d), BF16)],
        compiler_params=_params(("parallel", "arbitrary")),
    )(x, mod, w)


def _forget_cumsum_kernel(fl_ref, bias_ref, o_ref, carry):
    @pl.when(pl.program_id(0) == 0)
    def _():
        carry[...] = jnp.zeros_like(carry)

    tm = fl_ref.shape[0]
    log_f = -_softplus(-(fl_ref[...] + bias_ref[0:1, :]))
    row = lax.broadcasted_iota(jnp.int32, (tm, tm), 0)
    col = lax.broadcasted_iota(jnp.int32, (tm, tm), 1)
    tril = jnp.where(col <= row, 1.0, 0.0).astype(F32)
    c = _dot(tril, log_f, HI) + carry[0:1, :]
    o_ref[...] = c
    carry[...] = jnp.broadcast_to(c[tm - 1:tm, :], carry.shape)


def _forget_cumsum(fl, bias):
    s, n = fl.shape
    tm = min(256, s)
    return pl.pallas_call(
        _forget_cumsum_kernel,
        grid=(s // tm,),
        in_specs=[pl.BlockSpec((tm, n), lambda i: (i, 0)), pl.BlockSpec((8, n), lambda i: (0, 0))],
        out_specs=pl.BlockSpec((tm, n), lambda i: (i, 0)),
        out_shape=jax.ShapeDtypeStruct((s, n), F32),
        scratch_shapes=[pltpu.VMEM((8, n), F32)],
        compiler_params=_params(("arbitrary",)),
    )(fl, bias)


def _attn_kernel(q_ref, k_ref, v_ref, fq_ref, fk_ref, o_ref, m_scr, l_scr, acc_scr, *, tile):
    i = pl.program_id(1)
    lane = lax.broadcasted_iota(jnp.int32, (1, LANES), 1)
    first = lane < HEAD_DIM
    q = q_ref[...] * (HEAD_DIM ** -0.5)
    zero = jnp.zeros_like(q)
    q_heads = (jnp.where(first, q, zero), jnp.where(first, zero, q))
    row = lax.broadcasted_iota(jnp.int32, (tile, tile), 0)
    col = lax.broadcasted_iota(jnp.int32, (tile, tile), 1)

    m_scr[...] = jnp.full(m_scr.shape, NEG_BIG, F32)
    l_scr[...] = jnp.zeros_like(l_scr)
    acc_scr[...] = jnp.zeros_like(acc_scr)

    def step(j, masked):
        off = pl.multiple_of(j * tile, tile)
        kt = k_ref[pl.ds(off, tile), :]
        vt = v_ref[pl.ds(off, tile), :]
        for hh in range(2):
            fq = fq_ref[0, :, hh:hh + 1]
            fk = fk_ref[0, hh:hh + 1, pl.ds(off, tile)]
            s = _dot_nt(q_heads[hh], kt) + (fq - fk)
            if masked:
                s = jnp.where(col <= row, s, NEG_BIG)
            m_old = m_scr[hh]
            m_new = jnp.maximum(m_old, jnp.max(s, axis=-1, keepdims=True))
            scale = jnp.exp(m_old - m_new)
            p = jnp.exp(s - m_new)
            l_scr[hh] = scale * l_scr[hh] + jnp.sum(p, axis=-1, keepdims=True)
            acc_scr[hh] = scale * acc_scr[hh] + _dot(p.astype(BF16), vt)
            m_scr[hh] = m_new

    def body(j, carry):
        step(j, False)
        return carry

    lax.fori_loop(0, i, body, 0)
    step(i, True)
    o = jnp.where(first, acc_scr[0] / l_scr[0], acc_scr[1] / l_scr[1])
    o_ref[...] = o.astype(o_ref.dtype)


def _attention(qkv, f_cols, f_rows):
    s = qkv.shape[0]
    tile = min(512, s)
    n_pairs = N_HEADS // 2
    kcol = ATT_WIDTH // LANES
    return pl.pallas_call(
        functools.partial(_attn_kernel, tile=tile),
        grid=(n_pairs, s // tile),
        in_specs=[
            pl.BlockSpec((tile, LANES), lambda p, i: (i, p)),
            pl.BlockSpec((s, LANES), lambda p, i: (0, kcol + p)),
            pl.BlockSpec((s, LANES), lambda p, i: (0, 2 * kcol + p)),
            pl.BlockSpec((1, tile, 2), lambda p, i: (p, i, 0)),
            pl.BlockSpec((1, 2, s), lambda p, i: (p, 0, 0)),
        ],
        out_specs=pl.BlockSpec((tile, LANES), lambda p, i: (i, p)),
        out_shape=jax.ShapeDtypeStruct((s, ATT_WIDTH), BF16),
        scratch_shapes=[
            pltpu.VMEM((2, tile, 1), F32),
            pltpu.VMEM((2, tile, 1), F32),
            pltpu.VMEM((2, tile, LANES), F32),
        ],
        compiler_params=_params(("parallel", "arbitrary")),
    )(qkv, qkv, qkv, f_cols, f_rows)


def _lru_kernel(xb_ref, yb_ref, par_ref, cw_ref, ga_ref, gx_ref, o_ref,
                xpad_scr, a_scr, u_scr, h_scr, carry_scr):
    tm, w = xb_ref.shape

    @pl.when(pl.program_id(0) == 0)
    def _():
        xpad_scr[0:8, :] = jnp.zeros((8, w), F32)
        carry_scr[...] = jnp.zeros_like(carry_scr)

    xb = xb_ref[...]
    xpad_scr[8:8 + tm, :] = xb
    xc = xb * cw_ref[LRU_CONV - 1:LRU_CONV, :] + par_ref[0:1, :]
    for d in range(1, LRU_CONV):
        xc = xc + xpad_scr[8 - d:8 - d + tm, :] * cw_ref[LRU_CONV - 1 - d:LRU_CONV - d, :]
    xpad_scr[0:8, :] = xb[tm - 8:tm, :]

    xcb = xc.astype(BF16)
    r = _sigmoid(_dot(xcb, ga_ref[...]) + par_ref[1:2, :])
    gi = _sigmoid(_dot(xcb, gx_ref[...]) + par_ref[2:3, :])
    log_a = (-LRU_C) * r * _softplus(-par_ref[3:4, :])
    a = jnp.exp(log_a)
    u = jnp.sqrt(-jnp.tanh(log_a) * (a * a + 1.0)) * (gi * xc)

    r8 = lax.broadcasted_iota(jnp.int32, (tm, w), 0) % 8
    for d in (1, 2, 4):
        keep = r8 >= d
        a_prev = jnp.where(keep, pltpu.roll(a, d, axis=0), 1.0)
        u_prev = jnp.where(keep, pltpu.roll(u, d, axis=0), 0.0)
        u = a * u_prev + u
        a = a * a_prev
    a_scr[...] = a
    u_scr[...] = u

    def slab(k, carry):
        off = pl.multiple_of(k * 8, 8)
        h8 = a_scr[pl.ds(off, 8), :] * carry + u_scr[pl.ds(off, 8), :]
        h_scr[pl.ds(off, 8), :] = h8
        return jnp.broadcast_to(h8[7:8, :], (8, w))

    carry_scr[...] = lax.fori_loop(0, tm // 8, slab, carry_scr[...], unroll=4)

    yb = yb_ref[...]
    gelu = 0.5 * yb * (1.0 + jnp.tanh(0.7978845608028654 * (yb + 0.044715 * (yb * yb * yb))))
    o_ref[...] = (h_scr[...] * gelu).astype(o_ref.dtype)


def _lru(z_lru, par, conv_w, ga, gx):
    s = z_lru.shape[0]
    w = LRU_WIDTH
    tm = min(256, s)
    return pl.pallas_call(
        _lru_kernel,
        grid=(s // tm,),
        in_specs=[
            pl.BlockSpec((tm, w), lambda i: (i, 0)),
            pl.BlockSpec((tm, w), lambda i: (i, 1)),
            pl.BlockSpec((8, w), lambda i: (0, 0)),
            pl.BlockSpec((8, w), lambda i: (0, 0)),
            pl.BlockSpec((w, w), lambda i: (0, 0)),
            pl.BlockSpec((w, w), lambda i: (0, 0)),
        ],
        out_specs=pl.BlockSpec((tm, w), lambda i: (i, 0)),
        out_shape=jax.ShapeDtypeStruct((s, w), BF16),
        scratch_shapes=[
            pltpu.VMEM((tm + 8, w), F32),
            pltpu.VMEM((tm, w), F32),
            pltpu.VMEM((tm, w), F32),
            pltpu.VMEM((tm, w), F32),
            pltpu.VMEM((8, w), F32),
        ],
        compiler_params=_params(("arbitrary",)),
    )(z_lru, z_lru, par, conv_w, ga, gx)


def _rwkv_prep_kernel(z_ref, mu_ref, par_ref, w2_ref, a2_ref, g2_ref, blk_ref,
                      r_ref, ld_ref, k_ref, v_ref, an_ref, b_ref, bonus_ref, g_ref, prev_scr):
    tm = z_ref.shape[0]
    wd = RWKV_WIDTH

    @pl.when(pl.program_id(0) == 0)
    def _():
        prev_scr[...] = jnp.zeros_like(prev_scr)

    z = z_ref[...]
    row = lax.broadcasted_iota(jnp.int32, z.shape, 0)
    z_prev = jnp.where(row == 0, prev_scr[0:1, :], pltpu.roll(z, 1, axis=0))
    prev_scr[...] = jnp.broadcast_to(z[tm - 1:tm, :], prev_scr.shape)
    zs = z + (z_prev - z) * mu_ref[0:1, :]

    r = zs[:, 0:wd]
    k = zs[:, wd:2 * wd]
    v = zs[:, 2 * wd:3 * wd]
    wl = zs[:, 3 * wd:3 * wd + LORA_PAD]
    al = zs[:, 3 * wd + LORA_PAD:3 * wd + 2 * LORA_PAD]
    gl = zs[:, 3 * wd + 2 * LORA_PAD:3 * wd + 3 * LORA_PAD]
    w0, a0, k_k, k_a, r_k = (par_ref[n:n + 1, :] for n in range(5))

    w = -_softplus(-(w0 + _dot(jnp.tanh(wl), w2_ref[...], HI))) - 0.5
    a = _sigmoid(a0 + _dot(al, a2_ref[...], HI))
    g = _dot(_sigmoid(gl), g2_ref[...], HI)

    blk = blk_ref[...]
    kk = k * k_k
    norm = jnp.sqrt(_dot(kk * kk, blk, HI))
    kk = kk / jnp.maximum(norm, 1e-12)
    kf = k * (1.0 + (a - 1.0) * k_a)

    r_ref[...] = r
    ld_ref[...] = -jnp.exp(w)
    k_ref[...] = kf
    v_ref[...] = v
    an_ref[...] = -kk
    b_ref[...] = kk * a
    bonus_ref[...] = _dot(r * kf * r_k, blk, HI) * v
    g_ref[...] = g


def _rwkv_prep(zc, mu, par, w2, a2, g2, blk):
    s, n = zc.shape
    wd = RWKV_WIDTH
    tm = min(256, s)
    full = lambda shape: pl.BlockSpec(shape, lambda i: (0, 0))
    out = jax.ShapeDtypeStruct((s, wd), F32)
    return pl.pallas_call(
        _rwkv_prep_kernel,
        grid=(s // tm,),
        in_specs=[
            pl.BlockSpec((tm, n), lambda i: (i, 0)),
            full((8, n)), full((8, wd)), full((LORA_PAD, wd)), full((LORA_PAD, wd)),
            full((LORA_PAD, wd)), full((wd, wd)),
        ],
        out_specs=[pl.BlockSpec((tm, wd), lambda i: (i, 0))] * 8,
        out_shape=[out] * 8,
        scratch_shapes=[pltpu.VMEM((8, n), F32)],
        compiler_params=_params(("arbitrary",)),
    )(zc, mu, par, w2, a2, g2, blk)


def _stack_heads(x, first):
    zero = jnp.zeros_like(x)
    return jnp.concatenate([jnp.where(first, x, zero), jnp.where(first, zero, x)], axis=0)


def _rwkv_scan_kernel(r_ref, ld_ref, k_ref, v_ref, an_ref, b_ref, y_ref, state_scr):
    t = r_ref.shape[0]
    t2 = 2 * t

    @pl.when(pl.program_id(0) == 0)
    def _():
        state_scr[...] = jnp.zeros_like(state_scr)

    row_t = lax.broadcasted_iota(jnp.int32, (t, t), 0)
    col_t = lax.broadcasted_iota(jnp.int32, (t, t), 1)
    tril_t = jnp.where(col_t <= row_t, 1.0, 0.0).astype(F32)
    ld = ld_ref[...]
    c_incl = _dot(tril_t, ld, HI)
    p_incl = jnp.exp(c_incl)
    p_prev = jnp.exp(c_incl - ld)
    p_inv = jnp.exp(-c_incl)
    r_s = r_ref[...] * p_incl
    a_s = an_ref[...] * p_prev
    b_s = b_ref[...] * p_inv
    k_s = k_ref[...] * p_inv
    v_all = v_ref[...]

    lane = lax.broadcasted_iota(jnp.int32, (1, LANES), 1)
    first = lane < HEAD_DIM
    row2 = lax.broadcasted_iota(jnp.int32, (t2, t2), 0)
    col2 = lax.broadcasted_iota(jnp.int32, (t2, t2), 1)
    same_head = (row2 >= t) == (col2 >= t)
    strict = jnp.logical_and(same_head, (col2 % t) < (row2 % t))
    incl = jnp.logical_and(same_head, (col2 % t) <= (row2 % t))
    eye = jnp.where(row2 == col2, 1.0, 0.0).astype(F32)

    for p in range(N_HEADS // 2):
        sl = slice(p * LANES, (p + 1) * LANES)
        a2 = _stack_heads(a_s[:, sl], first)
        r2 = _stack_heads(r_s[:, sl], first)
        b2 = _stack_heads(b_s[:, sl], first)
        k2 = _stack_heads(k_s[:, sl], first)
        v2 = _stack_heads(v_all[:, sl], first)
        state = state_scr[p]

        m = _dot_nt(jnp.concatenate([a2, r2], axis=0), jnp.concatenate([b2, k2], axis=0), HI)
        l_ab = jnp.where(strict, m[0:t2, 0:t2], 0.0)
        l_ak = jnp.where(strict, m[0:t2, t2:2 * t2], 0.0)
        t_rb = jnp.where(incl, m[t2:2 * t2, 0:t2], 0.0)
        t_rk = jnp.where(incl, m[t2:2 * t2, t2:2 * t2], 0.0)

        inv = eye + l_ab
        power = l_ab
        steps = max(1, (t - 1).bit_length()) - 1
        for _ in range(steps):
            power = _dot(power, power, HI)
            inv = inv + _dot(inv, power, HI)

        rhs = _dot_nt(a2, state, HI) + _dot(l_ak, v2, HI)
        u2 = _dot(inv, rhs, HI)
        y2 = _dot_nt(r2, state, HI) + _dot(t_rb, u2, HI) + _dot(t_rk, v2, HI)
        y_ref[:, sl] = y2[0:t, :] + y2[t:t2, :]

        uv_t = jnp.concatenate([u2, v2], axis=0).T
        upd = _dot(uv_t, jnp.concatenate([b2, k2], axis=0), HI)
        state_scr[p] = (state + upd) * p_incl[t - 1:t, sl]


def _rwkv_scan(r, ld, k, v, an, b):
    s, wd = r.shape
    t = min(RWKV_CHUNK, s)
    spec = pl.BlockSpec((t, wd), lambda i: (i, 0))
    return pl.pallas_call(
        _rwkv_scan_kernel,
        grid=(s // t,),
        in_specs=[spec] * 6,
        out_specs=spec,
        out_shape=jax.ShapeDtypeStruct((s, wd), F32),
        scratch_shapes=[pltpu.VMEM((N_HEADS // 2, LANES, LANES), F32)],
        compiler_params=_params(("arbitrary",)),
    )(r, ld, k, v, an, b)


def _merge_kernel(x_ref, oa_ref, ob_ref, y_ref, bonus_ref, g_ref, ga_ref, gb_ref, gc_ref,
                  par_ref, gn_ref, blk_ref, wpa_ref, wpb_ref, wpc_ref, wo_ref, o_ref, *, alpha):
    blk = blk_ref[...] * (1.0 / HEAD_DIM)
    y = y_ref[...]
    mean = _dot(y, blk, HI)
    yc = y - mean
    var = _dot(yc * yc, blk, HI)
    yn = yc * lax.rsqrt(var + RWKV_GN_EPS) * gn_ref[0:1, :] + gn_ref[1:2, :]
    oc = ((yn + bonus_ref[...]) * g_ref[...]).astype(BF16)

    merged = (ga_ref[...] * _dot(oa_ref[...], wpa_ref[...])
              + gb_ref[...] * _dot(ob_ref[...], wpb_ref[...])
              + gc_ref[...] * _dot(oc, wpc_ref[...]))
    out = _dot(merged.astype(BF16), wo_ref[...])
    z = alpha * x_ref[...] + par_ref[0:1, :] * out
    o_ref[...] = _ln(z) * par_ref[1:2, :] + par_ref[2:3, :]


def _merge(x, oa, ob, y, bonus, g, gates, par, gn, blk, wpa, wpb, wpc, wo, alpha):
    s, d = x.shape
    wd = RWKV_WIDTH
    tm = min(256, s)
    rows = lambda width, col=0: pl.BlockSpec((tm, width), lambda i, col=col: (i, col))
    full = lambda shape: pl.BlockSpec(shape, lambda i: (0, 0))
    return pl.pallas_call(
        functools.partial(_merge_kernel, alpha=alpha),
        grid=(s // tm,),
        in_specs=[
            rows(d), rows(ATT_WIDTH), rows(LRU_WIDTH), rows(wd), rows(wd), rows(wd),
            rows(d, 0), rows(d, 1), rows(d, 2),
            full((8, d)), full((8, wd)), full((wd, wd)),
            full((ATT_WIDTH, d)), full((LRU_WIDTH, d)), full((wd, d)), full((d, d)),
        ],
        out_specs=rows(d),
        out_shape=jax.ShapeDtypeStruct((s, d), F32),
        compiler_params=_params(("parallel",)),
    )(x, oa, ob, y, bonus, g, gates, gates, gates, par, gn, blk, wpa, wpb, wpc, wo)


def _block_diag(w):
    nb, n, _ = w.shape
    eye = jnp.eye(nb, dtype=w.dtype)
    return (eye[:, None, :, None] * w[:, :, None, :]).reshape(nb * n, nb * n)


def kernel(x, c, ada_w, ada_b, ln_g, ln_b, ffn_up, ffn_down, w_in, fox_f_bias, lru_conv_w, lru_conv_b, lru_ga_w, lru_ga_b, lru_gx_w, lru_gx_b, lru_lambda, rwkv_mu, rwkv_w0, rwkv_w2, rwkv_a0, rwkv_a2, rwkv_g2, rwkv_k_k, rwkv_k_a, rwkv_r_k, rwkv_gn_w, rwkv_gn_b, w_proj_a, w_proj_b, w_proj_c, w_out):
    batch, s, d = x.shape
    assert batch == 1 and d == 1024
    depth = ada_w.shape[0]
    alpha = float((2 * depth) ** 0.25)
    wd = RWKV_WIDTH
    n_pairs = N_HEADS // 2

    ada = _ada_all(c, ada_w, ada_b)
    blk = _block_diag(jnp.ones((N_HEADS, HEAD_DIM, HEAD_DIM), F32))

    o_att = 3 * ATT_WIDTH
    o_lru = o_att + N_HEADS
    o_rwkv = o_lru + 2 * LRU_WIDTH
    o_gate = o_rwkv + 3 * wd + 64 + 64 + 128
    lora = 64

    xs = x[0]
    for l in range(depth):
        mods = [ada[l, n * d:(n + 1) * d] for n in range(9)]
        w_l = w_in[l]

        xs = _ffn(xs, _pack_rows(mods[0:3], d), _pack_rows([ln_g[l, 0], ln_b[l, 0]], d),
                  ffn_up[l, 0].astype(BF16), ffn_down[l, 0].astype(BF16), alpha)

        mod2 = _pack_rows(mods[3:5], d)
        qkv = _ln_matmul(xs, mod2, w_l[:, 0:o_att].astype(BF16), BF16)
        w_fl = jnp.pad(w_l[:, o_att:o_lru], ((0, 0), (0, LANES - N_HEADS))).astype(BF16)
        fl = _ln_matmul(xs, mod2, w_fl, F32)
        z_lru = _ln_matmul(xs, mod2, w_l[:, o_lru:o_rwkv].astype(BF16), F32)
        w_c = w_l[:, o_rwkv:o_gate]
        zpad = lambda n: jnp.zeros((d, n), F32)
        w_c = jnp.concatenate([
            w_c[:, 0:3 * wd],
            w_c[:, 3 * wd:3 * wd + lora], zpad(LORA_PAD - lora),
            w_c[:, 3 * wd + lora:3 * wd + 2 * lora], zpad(LORA_PAD - lora),
            w_c[:, 3 * wd + 2 * lora:], zpad(RWKV_COLS_PAD - 3 * wd - 3 * LORA_PAD)], axis=1)
        zc = _ln_matmul(xs, mod2, w_c.astype(BF16), F32)
        gates = _ln_matmul(xs, mod2, w_l[:, o_gate:].astype(BF16), F32, act="sigmoid")

        f_bias = _pack_rows([jnp.pad(fox_f_bias[l], (0, LANES - N_HEADS))], LANES)
        f_cum = _forget_cumsum(fl, f_bias)[:, 0:N_HEADS]
        f_cols = f_cum.reshape(s, n_pairs, 2).transpose(1, 0, 2)
        f_rows = f_cum.T.reshape(n_pairs, 2, s)
        o_a = _attention(qkv, f_cols, f_rows)

        lru_par = _pack_rows([lru_conv_b[l], lru_ga_b[l], lru_gx_b[l], lru_lambda[l]], LRU_WIDTH)
        conv_w = jnp.pad(lru_conv_w[l], ((0, 8 - LRU_CONV), (0, 0)))
        o_b = _lru(z_lru, lru_par, conv_w, _block_diag(lru_ga_w[l]).astype(BF16),
                   _block_diag(lru_gx_w[l]).astype(BF16))

        mu = rwkv_mu[l]
        zv = lambda n: jnp.zeros((n,), F32)
        mu_pad = jnp.concatenate([
            mu[0:3 * wd], mu[3 * wd:3 * wd + lora], zv(LORA_PAD - lora),
            mu[3 * wd + lora:3 * wd + 2 * lora], zv(LORA_PAD - lora),
            mu[3 * wd + 2 * lora:], zv(RWKV_COLS_PAD - 3 * wd - 3 * LORA_PAD)])
        rw_par = _pack_rows([rwkv_w0[l], rwkv_a0[l], rwkv_k_k[l], rwkv_k_a[l], rwkv_r_k[l].reshape(wd)], wd)
        pad_rows = lambda w: jnp.pad(w, ((0, LORA_PAD - w.shape[0]), (0, 0)))
        r, ld, kf, v, an, b, bonus, g = _rwkv_prep(
            zc, _pack_rows([mu_pad], RWKV_COLS_PAD), rw_par,
            pad_rows(rwkv_w2[l]), pad_rows(rwkv_a2[l]), rwkv_g2[l], blk)
        y = _rwkv_scan(r, ld, kf, v, an, b)

        par = _pack_rows([mods[5], ln_g[l, 1], ln_b[l, 1]], d)
        gn = _pack_rows([rwkv_gn_w[l], rwkv_gn_b[l]], wd)
        xs = _merge(xs, o_a, o_b, y, bonus, g, gates, par, gn, blk,
                    w_proj_a[l].astype(BF16), w_proj_b[l].astype(BF16), w_proj_c[l].astype(BF16),
                    w_out[l].astype(BF16), alpha)

        xs = _ffn(xs, _pack_rows(mods[6:9], d), _pack_rows([ln_g[l, 2], ln_b[l, 2]], d),
                  ffn_up[l, 1].astype(BF16), ffn_down[l, 1].astype(BF16), alpha)

    return xs[None]
```

```python
import functools

import jax
import jax.numpy as jnp
from jax import lax
from jax.experimental import pallas as pl
from jax.experimental.pallas import tpu as pltpu

F32 = jnp.float32
BF16 = jnp.bfloat16
HI = lax.Precision.HIGHEST

LANES = 128
HEAD_DIM = 64
N_HEADS = 8
ATT_WIDTH = 512
LRU_WIDTH = 1024
LRU_BLOCK = 64
LRU_CONV = 4
LRU_C = 8.0
RWKV_WIDTH = 512
LORA_PAD = 128
RWKV_COLS_PAD = 2048
RWKV_GN_EPS = 64e-5
LN_EPS = 1e-5
NEG_BIG = -1e30
VMEM_LIMIT = 56 * 1024 * 1024

RWKV_CHUNK = 64
ATT_TILE = 512
LOG2E = 1.4426950408889634

COL_QKV = 0
COL_FL = 1536
COL_LRU = 2048
COL_RWKV = 4096
COL_GATE = 6144
N_IN_PAD = 9216

LANE_ONE = HEAD_DIM
LANE_F = HEAD_DIM + 3


def _dot(a, b, prec=None):
    return jnp.dot(a, b, preferred_element_type=F32, precision=prec)


def _dot_nt(a, b, prec=None):
    return lax.dot_general(a, b, (((1,), (1,)), ((), ())), preferred_element_type=F32, precision=prec)


def _split2(x):
    hi = x.astype(BF16)
    return hi, (x - hi.astype(F32)).astype(BF16)


def _split3(x):
    hi = x.astype(BF16)
    r = x - hi.astype(F32)
    mid = r.astype(BF16)
    return hi, mid, (r - mid.astype(F32)).astype(BF16)


def _mm3(a, b):
    return _dot(a[0], b[0]) + (_dot(a[0], b[1]) + _dot(a[1], b[0]))


def _mm3_nt(a, b):
    return _dot_nt(a[0], b[0]) + (_dot_nt(a[0], b[1]) + _dot_nt(a[1], b[0]))


def _dot_x3(a, b):
    return _mm3(_split2(a), _split2(b))


def _dot_exact_lhs(sel, x):
    hi, mid, lo = _split3(x)
    return _dot(sel, hi) + (_dot(sel, mid) + _dot(sel, lo))


def _dot_exact_rhs(x, sel):
    hi, mid, lo = _split3(x)
    return _dot(hi, sel) + (_dot(mid, sel) + _dot(lo, sel))


def _ln(x):
    mu = jnp.mean(x, axis=-1, keepdims=True)
    xc = x - mu
    var = jnp.mean(xc * xc, axis=-1, keepdims=True)
    return xc * lax.rsqrt(var + LN_EPS)


def _softplus(x):
    return jnp.maximum(x, 0.0) + jnp.log1p(jnp.exp(-jnp.abs(x)))


def _sigmoid(x):
    return 1.0 / (1.0 + jnp.exp(-x))


def _params(sem):
    return pltpu.CompilerParams(dimension_semantics=sem, vmem_limit_bytes=VMEM_LIMIT)


def _pack_rows(rows, width):
    rows = [r.reshape(1, width).astype(F32) for r in rows]
    pad = jnp.zeros((8 - len(rows), width), F32)
    return jnp.concatenate(rows + [pad], axis=0)


def _ada_kernel(c_ref, w_ref, b_ref, o_ref):
    c = c_ref[...]
    ca = c * _sigmoid(c)
    o_ref[0] = _dot(ca, w_ref[0], HI) + b_ref[0]


def _ada_all(c, ada_w, ada_b):
    n_layers, d, n = ada_w.shape
    tn = 1152
    c8 = jnp.broadcast_to(c.astype(F32), (8, d))
    out = pl.pallas_call(
        _ada_kernel,
        grid=(n_layers, n // tn),
        in_specs=[
            pl.BlockSpec((8, d), lambda l, j: (0, 0)),
            pl.BlockSpec((1, d, tn), lambda l, j: (l, 0, j)),
            pl.BlockSpec((1, 1, tn), lambda l, j: (l, 0, j)),
        ],
        out_specs=pl.BlockSpec((1, 8, tn), lambda l, j: (l, 0, j)),
        out_shape=jax.ShapeDtypeStruct((n_layers, 8, n), F32),
        compiler_params=_params(("arbitrary", "arbitrary")),
    )(c8, ada_w, ada_b.reshape(n_layers, 1, n))
    return out[:, 0, :]


def _ffn_kernel(x_ref, mod_ref, lnp_ref, wu_ref, wg_ref, wd_ref, o_ref, h_scr, acc_scr, *, alpha):
    j = pl.program_id(1)

    @pl.when(j == 0)
    def _():
        h = _ln(x_ref[...]) * (1.0 + mod_ref[1:2, :]) + mod_ref[0:1, :]
        h_scr[...] = h.astype(BF16)
        acc_scr[...] = jnp.zeros_like(acc_scr)

    h = h_scr[...]
    u = _dot(h, wu_ref[...])
    g = _dot(h, wg_ref[...])
    act = (u * _sigmoid(u) * g).astype(BF16)
    acc_scr[...] += _dot(act, wd_ref[...])

    @pl.when(j == pl.num_programs(1) - 1)
    def _():
        y = (0.5 * mod_ref[2:3, :]) * acc_scr[...]
        z = alpha * x_ref[...] + y
        o_ref[...] = _ln(z) * lnp_ref[0:1, :] + lnp_ref[1:2, :]


def _ffn(x, mod, lnp, w_up, w_down, alpha):
    s, d = x.shape
    d_ff = w_down.shape[0]
    tm = min(1024, s)
    tf = 256
    nf = d_ff // tf
    return pl.pallas_call(
        functools.partial(_ffn_kernel, alpha=alpha),
        grid=(s // tm, nf),
        in_specs=[
            pl.BlockSpec((tm, d), lambda i, j: (i, 0)),
            pl.BlockSpec((8, d), lambda i, j: (0, 0)),
            pl.BlockSpec((8, d), lambda i, j: (0, 0)),
            pl.BlockSpec((d, tf), lambda i, j: (0, j)),
            pl.BlockSpec((d, tf), lambda i, j: (0, j + nf)),
            pl.BlockSpec((tf, d), lambda i, j: (j, 0)),
        ],
        out_specs=pl.BlockSpec((tm, d), lambda i, j: (i, 0)),
        out_shape=jax.ShapeDtypeStruct((s, d), F32),
        scratch_shapes=[pltpu.VMEM((tm, d), BF16), pltpu.VMEM((tm, d), F32)],
        compiler_params=_params(("parallel", "arbitrary")),
    )(x, mod, lnp, w_up, w_up, w_down)


def _in_proj_kernel(x_ref, mod_ref, w_ref, o_ref, h_scr, *, first_gate_tile):
    j = pl.program_id(1)

    @pl.when(j == 0)
    def _():
        h = _ln(x_ref[...]) * (1.0 + mod_ref[1:2, :]) + mod_ref[0:1, :]
        h_scr[...] = h.astype(BF16)

    z = _dot(h_scr[...], w_ref[...])

    @pl.when(j < first_gate_tile)
    def _():
        o_ref[...] = z

    @pl.when(j >= first_gate_tile)
    def _():
        o_ref[...] = _sigmoid(z)


def _in_proj(x, mod, w):
    s, d = x.shape
    n = w.shape[1]
    tm = min(1024, s)
    tn = 512
    return pl.pallas_call(
        functools.partial(_in_proj_kernel, first_gate_tile=COL_GATE // tn),
        grid=(s // tm, n // tn),
        in_specs=[
            pl.BlockSpec((tm, d), lambda i, j: (i, 0)),
            pl.BlockSpec((8, d), lambda i, j: (0, 0)),
            pl.BlockSpec((d, tn), lambda i, j: (0, j)),
        ],
        out_specs=pl.BlockSpec((tm, tn), lambda i, j: (i, j)),
        out_shape=jax.ShapeDtypeStruct((s, n), F32),
        scratch_shapes=[pltpu.VMEM((tm, d), BF16)],
        compiler_params=_params(("parallel", "arbitrary")),
    )(x, mod, w)


def _fox_pack_kernel(qkv_ref, fl_ref, bias_ref, qp_ref, kp_ref, vp_ref, fs_ref, carry):
    tm = fl_ref.shape[0]

    @pl.when(pl.program_id(0) == 0)
    def _():
        carry[...] = jnp.zeros_like(carry)

    log_f = -_softplus(-(fl_ref[...] + bias_ref[0:1, :]))
    row = lax.broadcasted_iota(jnp.int32, (tm, tm), 0)
    col = lax.broadcasted_iota(jnp.int32, (tm, tm), 1)
    tril = jnp.where(col <= row, 1.0, 0.0).astype(BF16)
    f_loc = _dot_exact_lhs(tril, log_f * LOG2E)

    start = carry[...]
    r8 = lax.broadcasted_iota(jnp.int32, (8, LANES), 0)
    c8 = lax.broadcasted_iota(jnp.int32, (8, LANES), 1)
    diag = jnp.where(r8 == c8, start, 0.0)
    fs_ref[0] = _dot_exact_rhs(diag, jnp.ones((LANES, LANES), BF16))
    carry[...] = start + f_loc[tm - 1:tm, :]

    lane = lax.broadcasted_iota(jnp.int32, (1, LANES), 1)
    head_lanes = lane < HEAD_DIM
    ones3 = jnp.logical_and(lane >= LANE_ONE, lane < LANE_ONE + 3).astype(F32)
    onesf = jnp.logical_and(lane >= LANE_F, lane < LANE_F + 3).astype(F32)
    one1 = (lane == LANE_ONE).astype(F32)
    for h in range(N_HEADS):
        f = jnp.broadcast_to(f_loc[:, h:h + 1], (tm, LANES))
        pieces = [p.astype(F32) for p in _split3(f)]
        q_extra = ones3
        k_extra = onesf
        for n, piece in enumerate(pieces):
            q_extra = q_extra + jnp.where(lane == LANE_F + n, piece, 0.0)
            k_extra = k_extra - jnp.where(lane == LANE_ONE + n, piece, 0.0)
        pair = h // 2
        heads = []
        for base in (0, ATT_WIDTH, 2 * ATT_WIDTH):
            xh = qkv_ref[:, base + pair * LANES:base + (pair + 1) * LANES]
            heads.append(pltpu.roll(xh, HEAD_DIM, axis=1) if h % 2 else xh)
        sl = slice(h * LANES, (h + 1) * LANES)
        qp_ref[:, sl] = jnp.where(head_lanes, heads[0] * (LOG2E * HEAD_DIM ** -0.5), q_extra).astype(BF16)
        kp_ref[:, sl] = jnp.where(head_lanes, heads[1], k_extra).astype(BF16)
        vp_ref[:, sl] = jnp.where(head_lanes, heads[2], one1).astype(BF16)


def _fox_pack(z_all, bias):
    s = z_all.shape[0]
    tm = min(ATT_TILE, s)
    wide = N_HEADS * LANES
    packed = jax.ShapeDtypeStruct((s, wide), BF16)
    return pl.pallas_call(
        _fox_pack_kernel,
        grid=(s // tm,),
        in_specs=[
            pl.BlockSpec((tm, 3 * ATT_WIDTH), lambda i: (i, COL_QKV // (3 * ATT_WIDTH))),
            pl.BlockSpec((tm, LANES), lambda i: (i, COL_FL // LANES)),
            pl.BlockSpec((8, LANES), lambda i: (0, 0)),
        ],
        out_specs=[pl.BlockSpec((tm, wide), lambda i: (i, 0))] * 3
        + [pl.BlockSpec((1, 8, LANES), lambda i: (i, 0, 0))],
        out_shape=[packed] * 3 + [jax.ShapeDtypeStruct((s // tm, 8, LANES), F32)],
        scratch_shapes=[pltpu.VMEM((8, LANES), F32)],
        compiler_params=_params(("arbitrary",)),
    )(z_all, z_all, bias)


def _attn_kernel(q_ref, k_ref, v_ref, fs_ref, o_ref, m_scr, acc_scr, *, tile):
    h = pl.program_id(0)
    i = pl.program_id(1)
    q = q_ref[...]
    reps = tile // LANES
    row = lax.broadcasted_iota(jnp.int32, (tile, tile), 0)
    col = lax.broadcasted_iota(jnp.int32, (tile, tile), 1)
    fs_q = fs_ref[i, pl.ds(h, 1), :]

    m_scr[...] = jnp.full(m_scr.shape, NEG_BIG, F32)
    acc_scr[...] = jnp.zeros_like(acc_scr)

    def step(j, masked):
        off = pl.multiple_of(j * tile, tile)
        s = _dot_nt(q, k_ref[pl.ds(off, tile), :])
        if masked:
            s = jnp.where(col <= row, s, NEG_BIG)
        shift = fs_q - fs_ref[j, pl.ds(h, 1), :]
        part = s[:, 0:LANES]
        for n in range(1, reps):
            part = jnp.maximum(part, s[:, n * LANES:(n + 1) * LANES])
        m_old = m_scr[...]
        m_new = jnp.maximum(m_old, jnp.max(part, axis=-1, keepdims=True) + shift)
        p = jnp.exp2(s - pltpu.repeat(m_new - shift, reps, axis=1))
        acc_scr[...] = jnp.exp2(m_old - m_new) * acc_scr[...] + _dot(p.astype(BF16), v_ref[pl.ds(off, tile), :])
        m_scr[...] = m_new

    def body(j, carry):
        step(j, False)
        return carry

    lax.fori_loop(0, i, body, 0)
    step(i, True)
    acc = acc_scr[...]
    lane = lax.broadcasted_iota(jnp.int32, (1, LANES), 1)
    o = jnp.where(lane < HEAD_DIM, acc / acc[:, LANE_ONE:LANE_ONE + 1], 0.0)
    o_ref[...] = o.astype(o_ref.dtype)


def _attention(qp, kp, vp, fs):
    s, wide = qp.shape
    tile = min(ATT_TILE, s)
    return pl.pallas_call(
        functools.partial(_attn_kernel, tile=tile),
        grid=(N_HEADS, s // tile),
        in_specs=[
            pl.BlockSpec((tile, LANES), lambda h, i: (i, h)),
            pl.BlockSpec((s, LANES), lambda h, i: (0, h)),
            pl.BlockSpec((s, LANES), lambda h, i: (0, h)),
            pl.BlockSpec(fs.shape, lambda h, i: (0, 0, 0)),
        ],
        out_specs=pl.BlockSpec((tile, LANES), lambda h, i: (i, h)),
        out_shape=jax.ShapeDtypeStruct((s, wide), BF16),
        scratch_shapes=[pltpu.VMEM((tile, LANES), F32), pltpu.VMEM((tile, LANES), F32)],
        compiler_params=_params(("parallel", "arbitrary")),
    )(qp, kp, vp, fs)


def _lru_kernel(xb_ref, yb_ref, par_ref, cw_ref, ga_ref, gx_ref, o_ref,
                xpad_scr, a_scr, u_scr, h_scr, carry_scr):
    tm, w = xb_ref.shape

    @pl.when(pl.program_id(0) == 0)
    def _():
        xpad_scr[0:8, :] = jnp.zeros((8, w), F32)
        carry_scr[...] = jnp.zeros_like(carry_scr)

    xb = xb_ref[...]
    xpad_scr[8:8 + tm, :] = xb
    xc = xb * cw_ref[LRU_CONV - 1:LRU_CONV, :] + par_ref[0:1, :]
    for d in range(1, LRU_CONV):
        xc = xc + xpad_scr[8 - d:8 - d + tm, :] * cw_ref[LRU_CONV - 1 - d:LRU_CONV - d, :]
    xpad_scr[0:8, :] = xb[tm - 8:tm, :]

    xcb = xc.astype(BF16)
    r = _sigmoid(_dot(xcb, ga_ref[...]) + par_ref[1:2, :])
    gi = _sigmoid(_dot(xcb, gx_ref[...]) + par_ref[2:3, :])
    log_a = (-LRU_C) * r * _softplus(-par_ref[3:4, :])
    a = jnp.exp(log_a)
    u = jnp.sqrt(-jnp.tanh(log_a) * (a * a + 1.0)) * (gi * xc)

    r8 = lax.broadcasted_iota(jnp.int32, (tm, w), 0) % 8
    for d in (1, 2, 4):
        keep = r8 >= d
        a_prev = jnp.where(keep, pltpu.roll(a, d, axis=0), 1.0)
        u_prev = jnp.where(keep, pltpu.roll(u, d, axis=0), 0.0)
        u = a * u_prev + u
        a = a * a_prev
    a_scr[...] = a
    u_scr[...] = u

    def slab(k, carry):
        off = pl.multiple_of(k * 8, 8)
        h8 = a_scr[pl.ds(off, 8), :] * carry + u_scr[pl.ds(off, 8), :]
        h_scr[pl.ds(off, 8), :] = h8
        return jnp.broadcast_to(h8[7:8, :], (8, w))

    carry_scr[...] = lax.fori_loop(0, tm // 8, slab, carry_scr[...], unroll=4)

    yb = yb_ref[...]
    gelu = 0.5 * yb * (1.0 + jnp.tanh(0.7978845608028654 * (yb + 0.044715 * (yb * yb * yb))))
    o_ref[...] = (h_scr[...] * gelu).astype(o_ref.dtype)


def _lru(z_lru, par, conv_w, ga, gx):
    s = z_lru.shape[0]
    w = LRU_WIDTH
    tm = min(256, s)
    return pl.pallas_call(
        _lru_kernel,
        grid=(s // tm,),
        in_specs=[
            pl.BlockSpec((tm, w), lambda i: (i, COL_LRU // w)),
            pl.BlockSpec((tm, w), lambda i: (i, COL_LRU // w + 1)),
            pl.BlockSpec((8, w), lambda i: (0, 0)),
            pl.BlockSpec((8, w), lambda i: (0, 0)),
            pl.BlockSpec((w, w), lambda i: (0, 0)),
            pl.BlockSpec((w, w), lambda i: (0, 0)),
        ],
        out_specs=pl.BlockSpec((tm, w), lambda i: (i, 0)),
        out_shape=jax.ShapeDtypeStruct((s, w), BF16),
        scratch_shapes=[
            pltpu.VMEM((tm + 8, w), F32),
            pltpu.VMEM((tm, w), F32),
            pltpu.VMEM((tm, w), F32),
            pltpu.VMEM((tm, w), F32),
            pltpu.VMEM((8, w), F32),
        ],
        compiler_params=_params(("arbitrary",)),
    )(z_lru, z_lru, par, conv_w, ga, gx)


def _rwkv_prep_kernel(z_ref, mu_ref, par_ref, w2_ref, a2_ref, g2_ref, blk_ref,
                      r_ref, ld_ref, k_ref, v_ref, an_ref, b_ref, bonus_ref, g_ref, prev_scr):
    tm = z_ref.shape[0]
    wd = RWKV_WIDTH

    @pl.when(pl.program_id(0) == 0)
    def _():
        prev_scr[...] = jnp.zeros_like(prev_scr)

    z = z_ref[...]
    row = lax.broadcasted_iota(jnp.int32, z.shape, 0)
    z_prev = jnp.where(row == 0, prev_scr[0:1, :], pltpu.roll(z, 1, axis=0))
    prev_scr[...] = jnp.broadcast_to(z[tm - 1:tm, :], prev_scr.shape)
    zs = z + (z_prev - z) * mu_ref[0:1, :]

    r = zs[:, 0:wd]
    k = zs[:, wd:2 * wd]
    v = zs[:, 2 * wd:3 * wd]
    wl = zs[:, 3 * wd:3 * wd + LORA_PAD]
    al = zs[:, 3 * wd + LORA_PAD:3 * wd + 2 * LORA_PAD]
    gl = zs[:, 3 * wd + 2 * LORA_PAD:3 * wd + 3 * LORA_PAD]
    w0, a0, k_k, k_a, r_k = (par_ref[n:n + 1, :] for n in range(5))

    w = -_softplus(-(w0 + _dot_x3(jnp.tanh(wl), w2_ref[...]))) - 0.5
    a = _sigmoid(a0 + _dot_x3(al, a2_ref[...]))
    g = _dot_x3(_sigmoid(gl), g2_ref[...])

    blk = blk_ref[...]
    kk = k * k_k
    norm = jnp.sqrt(_dot_exact_rhs(kk * kk, blk))
    kk = kk / jnp.maximum(norm, 1e-12)
    kf = k * (1.0 + (a - 1.0) * k_a)

    r_ref[...] = r
    ld_ref[...] = -jnp.exp(w)
    k_ref[...] = kf
    v_ref[...] = v
    an_ref[...] = -kk
    b_ref[...] = kk * a
    bonus_ref[...] = _dot_exact_rhs(r * kf * r_k, blk) * v
    g_ref[...] = g


def _rwkv_prep(zc, mu, par, w2, a2, g2, blk):
    s = zc.shape[0]
    n = RWKV_COLS_PAD
    wd = RWKV_WIDTH
    tm = min(256, s)
    full = lambda shape: pl.BlockSpec(shape, lambda i: (0, 0))
    out = jax.ShapeDtypeStruct((s, wd), F32)
    return pl.pallas_call(
        _rwkv_prep_kernel,
        grid=(s // tm,),
        in_specs=[
            pl.BlockSpec((tm, n), lambda i: (i, COL_RWKV // n)),
            full((8, n)), full((8, wd)), full((LORA_PAD, wd)), full((LORA_PAD, wd)),
            full((LORA_PAD, wd)), full((wd, wd)),
        ],
        out_specs=[pl.BlockSpec((tm, wd), lambda i: (i, 0))] * 8,
        out_shape=[out] * 8,
        scratch_shapes=[pltpu.VMEM((8, n), F32)],
        compiler_params=_params(("arbitrary",)),
    )(zc, mu, par, w2, a2, g2, blk)


def _stack_heads(x, first):
    zero = jnp.zeros_like(x)
    return jnp.concatenate([jnp.where(first, x, zero), jnp.where(first, zero, x)], axis=0)


def _rwkv_scan_kernel(r_ref, ld_ref, k_ref, v_ref, an_ref, b_ref, y_ref, state_scr):
    t = r_ref.shape[0]
    t2 = 2 * t

    @pl.when(pl.program_id(0) == 0)
    def _():
        state_scr[...] = jnp.zeros_like(state_scr)

    row_t = lax.broadcasted_iota(jnp.int32, (t, t), 0)
    col_t = lax.broadcasted_iota(jnp.int32, (t, t), 1)
    tril_t = jnp.where(col_t <= row_t, 1.0, 0.0).astype(BF16)
    ld = ld_ref[...]
    c_incl = _dot_exact_lhs(tril_t, ld)
    p_incl = jnp.exp(c_incl)
    p_prev = jnp.exp(c_incl - ld)
    p_inv = jnp.exp(-c_incl)
    r_s = r_ref[...] * p_incl
    a_s = an_ref[...] * p_prev
    b_s = b_ref[...] * p_inv
    k_s = k_ref[...] * p_inv
    v_all = v_ref[...]

    lane = lax.broadcasted_iota(jnp.int32, (1, LANES), 1)
    first = lane < HEAD_DIM
    row2 = lax.broadcasted_iota(jnp.int32, (t2, t2), 0)
    col2 = lax.broadcasted_iota(jnp.int32, (t2, t2), 1)
    same_head = (row2 >= t) == (col2 >= t)
    strict = jnp.logical_and(same_head, (col2 % t) < (row2 % t))
    incl = jnp.logical_and(same_head, (col2 % t) <= (row2 % t))
    eye = jnp.where(row2 == col2, 1.0, 0.0).astype(F32)

    for p in range(N_HEADS // 2):
        sl = slice(p * LANES, (p + 1) * LANES)
        a2 = _stack_heads(a_s[:, sl], first)
        r2 = _stack_heads(r_s[:, sl], first)
        b2 = _stack_heads(b_s[:, sl], first)
        k2 = _stack_heads(k_s[:, sl], first)
        v2 = _stack_heads(v_all[:, sl], first)
        state = state_scr[p]

        a2_s, r2_s = _split2(a2), _split2(r2)
        ar_s = (a2_s, r2_s)
        bk_s = _split2(jnp.concatenate([b2, k2], axis=0))
        m = _mm3_nt(tuple(jnp.concatenate([x, y], axis=0) for x, y in zip(a2_s, r2_s)), bk_s)
        l_ab = jnp.where(strict, m[0:t2, 0:t2], 0.0)
        l_ak = jnp.where(strict, m[0:t2, t2:2 * t2], 0.0)
        t_rb = jnp.where(incl, m[t2:2 * t2, 0:t2], 0.0)
        t_rk = jnp.where(incl, m[t2:2 * t2, t2:2 * t2], 0.0)

        inv = eye + l_ab
        power = l_ab
        steps = max(1, (t - 1).bit_length()) - 1
        for _ in range(steps):
            power_s = _split2(power)
            power = _mm3(power_s, power_s)
            inv = inv + _mm3(_split2(inv), _split2(power))

        state_s = _split2(state)
        v2_s = _split2(v2)
        rhs = _mm3_nt(ar_s[0], state_s) + _mm3(_split2(l_ak), v2_s)
        u2 = _mm3(_split2(inv), _split2(rhs))
        y2 = _mm3_nt(ar_s[1], state_s) + _mm3(_split2(t_rb), _split2(u2)) + _mm3(_split2(t_rk), v2_s)
        y_ref[:, sl] = y2[0:t, :] + y2[t:t2, :]

        uv_t = jnp.concatenate([u2, v2], axis=0).T
        upd = _mm3(_split2(uv_t), bk_s)
        state_scr[p] = (state + upd) * p_incl[t - 1:t, sl]


def _rwkv_scan(r, ld, k, v, an, b):
    s, wd = r.shape
    t = min(RWKV_CHUNK, s)
    spec = pl.BlockSpec((t, wd), lambda i: (i, 0))
    return pl.pallas_call(
        _rwkv_scan_kernel,
        grid=(s // t,),
        in_specs=[spec] * 6,
        out_specs=spec,
        out_shape=jax.ShapeDtypeStruct((s, wd), F32),
        scratch_shapes=[pltpu.VMEM((N_HEADS // 2, LANES, LANES), F32)],
        compiler_params=_params(("arbitrary",)),
    )(r, ld, k, v, an, b)


def _merge_kernel(x_ref, oa_ref, ob_ref, y_ref, bonus_ref, g_ref, ga_ref, gb_ref, gc_ref,
                  par_ref, gn_ref, blk_ref, wpa_ref, wpb_ref, wpc_ref, wo_ref, o_ref, *, alpha):
    blk = blk_ref[...]
    y = y_ref[...]
    mean = _dot_exact_rhs(y, blk) * (1.0 / HEAD_DIM)
    yc = y - mean
    var = _dot_exact_rhs(yc * yc, blk) * (1.0 / HEAD_DIM)
    yn = yc * lax.rsqrt(var + RWKV_GN_EPS) * gn_ref[0:1, :] + gn_ref[1:2, :]
    oc = ((yn + bonus_ref[...]) * g_ref[...]).astype(BF16)

    merged = (ga_ref[...] * _dot(oa_ref[...], wpa_ref[...])
              + gb_ref[...] * _dot(ob_ref[...], wpb_ref[...])
              + gc_ref[...] * _dot(oc, wpc_ref[...]))
    out = _dot(merged.astype(BF16), wo_ref[...])
    z = alpha * x_ref[...] + par_ref[0:1, :] * out
    o_ref[...] = _ln(z) * par_ref[1:2, :] + par_ref[2:3, :]


def _merge(x, oa, ob, y, bonus, g, z_all, par, gn, blk, wpa, wpb, wpc, wo, alpha):
    s, d = x.shape
    wd = RWKV_WIDTH
    tm = min(256, s)
    rows = lambda width, col=0: pl.BlockSpec((tm, width), lambda i, col=col: (i, col))
    full = lambda shape: pl.BlockSpec(shape, lambda i: (0, 0))
    gate0 = COL_GATE // d
    return pl.pallas_call(
        functools.partial(_merge_kernel, alpha=alpha),
        grid=(s // tm,),
        in_specs=[
            rows(d), rows(oa.shape[1]), rows(LRU_WIDTH), rows(wd), rows(wd), rows(wd),
            rows(d, gate0), rows(d, gate0 + 1), rows(d, gate0 + 2),
            full((8, d)), full((8, wd)), full((wd, wd)),
            full(wpa.shape), full((LRU_WIDTH, d)), full((wd, d)), full((d, d)),
        ],
        out_specs=rows(d),
        out_shape=jax.ShapeDtypeStruct((s, d), F32),
        compiler_params=_params(("parallel",)),
    )(x, oa, ob, y, bonus, g, z_all, z_all, z_all, par, gn, blk, wpa, wpb, wpc, wo)


def _block_diag(w):
    nb, n, _ = w.shape
    eye = jnp.eye(nb, dtype=w.dtype)
    return (eye[:, None, :, None] * w[:, :, None, :]).reshape(nb * n, nb * n)


def kernel(x, c, ada_w, ada_b, ln_g, ln_b, ffn_up, ffn_down, w_in, fox_f_bias, lru_conv_w, lru_conv_b, lru_ga_w, lru_ga_b, lru_gx_w, lru_gx_b, lru_lambda, rwkv_mu, rwkv_w0, rwkv_w2, rwkv_a0, rwkv_a2, rwkv_g2, rwkv_k_k, rwkv_k_a, rwkv_r_k, rwkv_gn_w, rwkv_gn_b, w_proj_a, w_proj_b, w_proj_c, w_out):
    batch, s, d = x.shape
    assert batch == 1 and d == 1024
    depth = ada_w.shape[0]
    alpha = float((2 * depth) ** 0.25)
    wd = RWKV_WIDTH

    ada = _ada_all(c, ada_w, ada_b)
    blk = _block_diag(jnp.ones((N_HEADS, HEAD_DIM, HEAD_DIM), BF16))

    o_att = 3 * ATT_WIDTH
    o_lru = o_att + N_HEADS
    o_rwkv = o_lru + 2 * LRU_WIDTH
    o_gate = o_rwkv + 3 * wd + 64 + 64 + 128
    lora = 64

    xs = x[0]
    for l in range(depth):
        mods = [ada[l, n * d:(n + 1) * d] for n in range(9)]
        w_l = w_in[l]

        xs = _ffn(xs, _pack_rows(mods[0:3], d), _pack_rows([ln_g[l, 0], ln_b[l, 0]], d),
                  ffn_up[l, 0].astype(BF16), ffn_down[l, 0].astype(BF16), alpha)

        w_c = w_l[:, o_rwkv:o_gate]
        zpad = lambda n: jnp.zeros((d, n), F32)
        w_all = jnp.concatenate([
            w_l[:, 0:o_att],
            w_l[:, o_att:o_lru], zpad(COL_LRU - COL_FL - N_HEADS),
            w_l[:, o_lru:o_rwkv],
            w_c[:, 0:3 * wd],
            w_c[:, 3 * wd:3 * wd + lora], zpad(LORA_PAD - lora),
            w_c[:, 3 * wd + lora:3 * wd + 2 * lora], zpad(LORA_PAD - lora),
            w_c[:, 3 * wd + 2 * lora:], zpad(RWKV_COLS_PAD - 3 * wd - 3 * LORA_PAD),
            w_l[:, o_gate:]], axis=1).astype(BF16)
        assert w_all.shape[1] == N_IN_PAD
        z_all = _in_proj(xs, _pack_rows(mods[3:5], d), w_all)

        f_bias = _pack_rows([jnp.pad(fox_f_bias[l], (0, LANES - N_HEADS))], LANES)
        qp, kp, vp, fs = _fox_pack(z_all, f_bias)
        o_a = _attention(qp, kp, vp, fs)
        wpa = jnp.pad(w_proj_a[l].reshape(N_HEADS, HEAD_DIM, d), ((0, 0), (0, LANES - HEAD_DIM), (0, 0)))
        wpa = wpa.reshape(N_HEADS * LANES, d).astype(BF16)

        lru_par = _pack_rows([lru_conv_b[l], lru_ga_b[l], lru_gx_b[l], lru_lambda[l]], LRU_WIDTH)
        conv_w = jnp.pad(lru_conv_w[l], ((0, 8 - LRU_CONV), (0, 0)))
        o_b = _lru(z_all, lru_par, conv_w, _block_diag(lru_ga_w[l]).astype(BF16),
                   _block_diag(lru_gx_w[l]).astype(BF16))

        mu = rwkv_mu[l]
        zv = lambda n: jnp.zeros((n,), F32)
        mu_pad = jnp.concatenate([
            mu[0:3 * wd], mu[3 * wd:3 * wd + lora], zv(LORA_PAD - lora),
            mu[3 * wd + lora:3 * wd + 2 * lora], zv(LORA_PAD - lora),
            mu[3 * wd + 2 * lora:], zv(RWKV_COLS_PAD - 3 * wd - 3 * LORA_PAD)])
        rw_par = _pack_rows([rwkv_w0[l], rwkv_a0[l], rwkv_k_k[l], rwkv_k_a[l], rwkv_r_k[l].reshape(wd)], wd)
        pad_rows = lambda w: jnp.pad(w, ((0, LORA_PAD - w.shape[0]), (0, 0)))
        r, ld, kf, v, an, b, bonus, g = _rwkv_prep(
            z_all, _pack_rows([mu_pad], RWKV_COLS_PAD), rw_par,
            pad_rows(rwkv_w2[l]), pad_rows(rwkv_a2[l]), rwkv_g2[l], blk)
        y = _rwkv_scan(r, ld, kf, v, an, b)

        par = _pack_rows([mods[5], ln_g[l, 1], ln_b[l, 1]], d)
        gn = _pack_rows([rwkv_gn_w[l], rwkv_gn_b[l]], wd)
        xs = _merge(xs, o_a, o_b, y, bonus, g, z_all, par, gn, blk,
                    wpa, w_proj_b[l].astype(BF16), w_proj_c[l].astype(BF16),
                    w_out[l].astype(BF16), alpha)

        xs = _ffn(xs, _pack_rows(mods[6:9], d), _pack_rows([ln_g[l, 2], ln_b[l, 2]], d),
                  ffn_up[l, 1].astype(BF16), ffn_down[l, 1].astype(BF16), alpha)

    return xs[None]
```

```python
import functools

import jax
import jax.numpy as jnp
from jax import lax
from jax.experimental import pallas as pl
from jax.experimental.pallas import tpu as pltpu

F32 = jnp.float32
BF16 = jnp.bfloat16
HI = lax.Precision.HIGHEST

LANES = 128
HEAD_DIM = 64
N_HEADS = 8
ATT_WIDTH = 512
LRU_WIDTH = 1024
LRU_BLOCK = 64
LRU_CONV = 4
LRU_C = 8.0
RWKV_WIDTH = 512
LORA_PAD = 128
RWKV_COLS_PAD = 2048
RWKV_GN_EPS = 64e-5
LN_EPS = 1e-5
NEG_BIG = -1e30
VMEM_LIMIT = 56 * 1024 * 1024

RWKV_CHUNK = 64
SCAN_GROUP_HEADS = 2
ATT_TILE = 512
LOG2E = 1.4426950408889634

COL_QKV = 0
COL_FL = 1536
COL_LRU = 2048
COL_RWKV = 4096
COL_GATE = 6144
N_IN_PAD = 9216

LANE_ONE = HEAD_DIM
LANE_F = HEAD_DIM + 3


def _dot(a, b, prec=None):
    return jnp.dot(a, b, preferred_element_type=F32, precision=prec)


def _dot_nt(a, b, prec=None):
    return lax.dot_general(a, b, (((1,), (1,)), ((), ())), preferred_element_type=F32, precision=prec)


def _split2(x):
    hi = x.astype(BF16)
    return hi, (x - hi.astype(F32)).astype(BF16)


def _split3(x):
    hi = x.astype(BF16)
    r = x - hi.astype(F32)
    mid = r.astype(BF16)
    return hi, mid, (r - mid.astype(F32)).astype(BF16)


def _mm3(a, b):
    cross = _dot(jnp.concatenate([a[0], a[1]], axis=1), jnp.concatenate([b[1], b[0]], axis=0))
    return _dot(a[0], b[0]) + cross


def _mm3_nt(a, b):
    cross = _dot_nt(jnp.concatenate([a[0], a[1]], axis=1), jnp.concatenate([b[1], b[0]], axis=1))
    return _dot_nt(a[0], b[0]) + cross


def _dot_x3(a, b):
    return _mm3(_split2(a), _split2(b))


def _dot_exact_lhs(sel, x):
    hi, mid, lo = _split3(x)
    return _dot(sel, hi) + (_dot(sel, mid) + _dot(sel, lo))


def _dot_exact_rhs(x, sel):
    hi, mid, lo = _split3(x)
    return _dot(hi, sel) + (_dot(mid, sel) + _dot(lo, sel))


def _ln(x):
    mu = jnp.mean(x, axis=-1, keepdims=True)
    xc = x - mu
    var = jnp.mean(xc * xc, axis=-1, keepdims=True)
    return xc * lax.rsqrt(var + LN_EPS)


def _softplus(x):
    return jnp.maximum(x, 0.0) + jnp.log1p(jnp.exp(-jnp.abs(x)))


def _sigmoid(x):
    return 1.0 / (1.0 + jnp.exp(-x))


def _params(sem):
    return pltpu.CompilerParams(dimension_semantics=sem, vmem_limit_bytes=VMEM_LIMIT)


def _pack_rows(rows, width):
    rows = [r.reshape(1, width).astype(F32) for r in rows]
    pad = jnp.zeros((8 - len(rows), width), F32)
    return jnp.concatenate(rows + [pad], axis=0)


def _ada_kernel(c_ref, w_ref, b_ref, o_ref):
    c = c_ref[...]
    ca = c * _sigmoid(c)
    o_ref[0] = _dot(ca, w_ref[0], HI) + b_ref[0]


def _ada_all(c, ada_w, ada_b):
    n_layers, d, n = ada_w.shape
    tn = 1152
    c8 = jnp.broadcast_to(c.astype(F32), (8, d))
    out = pl.pallas_call(
        _ada_kernel,
        grid=(n_layers, n // tn),
        in_specs=[
            pl.BlockSpec((8, d), lambda l, j: (0, 0)),
            pl.BlockSpec((1, d, tn), lambda l, j: (l, 0, j)),
            pl.BlockSpec((1, 1, tn), lambda l, j: (l, 0, j)),
        ],
        out_specs=pl.BlockSpec((1, 8, tn), lambda l, j: (l, 0, j)),
        out_shape=jax.ShapeDtypeStruct((n_layers, 8, n), F32),
        compiler_params=_params(("arbitrary", "arbitrary")),
    )(c8, ada_w, ada_b.reshape(n_layers, 1, n))
    return out[:, 0, :]


def _ffn_kernel(x_ref, mod_ref, lnp_ref, wu_ref, wg_ref, wd_ref, o_ref, h_scr, acc_scr, *, alpha):
    j = pl.program_id(1)

    @pl.when(j == 0)
    def _():
        h = _ln(x_ref[...]) * (1.0 + mod_ref[1:2, :]) + mod_ref[0:1, :]
        h_scr[...] = h.astype(BF16)
        acc_scr[...] = jnp.zeros_like(acc_scr)

    h = h_scr[...]
    u = _dot(h, wu_ref[...])
    g = _dot(h, wg_ref[...])
    act = (u * _sigmoid(u) * g).astype(BF16)
    acc_scr[...] += _dot(act, wd_ref[...])

    @pl.when(j == pl.num_programs(1) - 1)
    def _():
        y = (0.5 * mod_ref[2:3, :]) * acc_scr[...]
        z = alpha * x_ref[...] + y
        o_ref[...] = _ln(z) * lnp_ref[0:1, :] + lnp_ref[1:2, :]


def _ffn(x, mod, lnp, w_up, w_down, alpha):
    s, d = x.shape
    d_ff = w_down.shape[0]
    tm = min(1024, s)
    tf = 256
    nf = d_ff // tf
    return pl.pallas_call(
        functools.partial(_ffn_kernel, alpha=alpha),
        grid=(s // tm, nf),
        in_specs=[
            pl.BlockSpec((tm, d), lambda i, j: (i, 0)),
            pl.BlockSpec((8, d), lambda i, j: (0, 0)),
            pl.BlockSpec((8, d), lambda i, j: (0, 0)),
            pl.BlockSpec((d, tf), lambda i, j: (0, j)),
            pl.BlockSpec((d, tf), lambda i, j: (0, j + nf)),
            pl.BlockSpec((tf, d), lambda i, j: (j, 0)),
        ],
        out_specs=pl.BlockSpec((tm, d), lambda i, j: (i, 0)),
        out_shape=jax.ShapeDtypeStruct((s, d), F32),
        scratch_shapes=[pltpu.VMEM((tm, d), BF16), pltpu.VMEM((tm, d), F32)],
        compiler_params=_params(("parallel", "arbitrary")),
    )(x, mod, lnp, w_up, w_up, w_down)


def _in_proj_kernel(x_ref, mod_ref, w_ref, o_ref, h_scr, *, first_gate_tile):
    j = pl.program_id(1)

    @pl.when(j == 0)
    def _():
        h = _ln(x_ref[...]) * (1.0 + mod_ref[1:2, :]) + mod_ref[0:1, :]
        h_scr[...] = h.astype(BF16)

    z = _dot(h_scr[...], w_ref[...])

    @pl.when(j < first_gate_tile)
    def _():
        o_ref[...] = z

    @pl.when(j >= first_gate_tile)
    def _():
        o_ref[...] = _sigmoid(z)


def _in_proj(x, mod, w):
    s, d = x.shape
    n = w.shape[1]
    tm = min(1024, s)
    tn = 1024
    return pl.pallas_call(
        functools.partial(_in_proj_kernel, first_gate_tile=COL_GATE // tn),
        grid=(s // tm, n // tn),
        in_specs=[
            pl.BlockSpec((tm, d), lambda i, j: (i, 0)),
            pl.BlockSpec((8, d), lambda i, j: (0, 0)),
            pl.BlockSpec((d, tn), lambda i, j: (0, j)),
        ],
        out_specs=pl.BlockSpec((tm, tn), lambda i, j: (i, j)),
        out_shape=jax.ShapeDtypeStruct((s, n), F32),
        scratch_shapes=[pltpu.VMEM((tm, d), BF16)],
        compiler_params=_params(("parallel", "arbitrary")),
    )(x, mod, w)


def _fox_pack_kernel(qkv_ref, fl_ref, bias_ref, qp_ref, kp_ref, vp_ref, fs_ref, carry):
    tm = fl_ref.shape[0]

    @pl.when(pl.program_id(0) == 0)
    def _():
        carry[...] = jnp.zeros_like(carry)

    log_f = -_softplus(-(fl_ref[...] + bias_ref[0:1, :]))
    row = lax.broadcasted_iota(jnp.int32, (tm, tm), 0)
    col = lax.broadcasted_iota(jnp.int32, (tm, tm), 1)
    tril = jnp.where(col <= row, 1.0, 0.0).astype(BF16)
    f_loc = _dot_exact_lhs(tril, log_f * LOG2E)

    start = carry[...]
    r8 = lax.broadcasted_iota(jnp.int32, (8, LANES), 0)
    c8 = lax.broadcasted_iota(jnp.int32, (8, LANES), 1)
    diag = jnp.where(r8 == c8, start, 0.0)
    fs_ref[0] = _dot_exact_rhs(diag, jnp.ones((LANES, LANES), BF16))
    carry[...] = start + f_loc[tm - 1:tm, :]

    lane = lax.broadcasted_iota(jnp.int32, (1, LANES), 1)
    head_lanes = lane < HEAD_DIM
    ones3 = jnp.logical_and(lane >= LANE_ONE, lane < LANE_ONE + 3).astype(F32)
    onesf = jnp.logical_and(lane >= LANE_F, lane < LANE_F + 3).astype(F32)
    one1 = (lane == LANE_ONE).astype(F32)
    for h in range(N_HEADS):
        f = jnp.broadcast_to(f_loc[:, h:h + 1], (tm, LANES))
        pieces = [p.astype(F32) for p in _split3(f)]
        q_extra = ones3
        k_extra = onesf
        for n, piece in enumerate(pieces):
            q_extra = q_extra + jnp.where(lane == LANE_F + n, piece, 0.0)
            k_extra = k_extra - jnp.where(lane == LANE_ONE + n, piece, 0.0)
        pair = h // 2
        heads = []
        for base in (0, ATT_WIDTH, 2 * ATT_WIDTH):
            xh = qkv_ref[:, base + pair * LANES:base + (pair + 1) * LANES]
            heads.append(pltpu.roll(xh, HEAD_DIM, axis=1) if h % 2 else xh)
        sl = slice(h * LANES, (h + 1) * LANES)
        qp_ref[:, sl] = jnp.where(head_lanes, heads[0] * (LOG2E * HEAD_DIM ** -0.5), q_extra).astype(BF16)
        kp_ref[:, sl] = jnp.where(head_lanes, heads[1], k_extra).astype(BF16)
        vp_ref[:, sl] = jnp.where(head_lanes, heads[2], one1).astype(BF16)


def _fox_pack(z_all, bias):
    s = z_all.shape[0]
    tm = min(ATT_TILE, s)
    wide = N_HEADS * LANES
    packed = jax.ShapeDtypeStruct((s, wide), BF16)
    return pl.pallas_call(
        _fox_pack_kernel,
        grid=(s // tm,),
        in_specs=[
            pl.BlockSpec((tm, 3 * ATT_WIDTH), lambda i: (i, COL_QKV // (3 * ATT_WIDTH))),
            pl.BlockSpec((tm, LANES), lambda i: (i, COL_FL // LANES)),
            pl.BlockSpec((8, LANES), lambda i: (0, 0)),
        ],
        out_specs=[pl.BlockSpec((tm, wide), lambda i: (i, 0))] * 3
        + [pl.BlockSpec((1, 8, LANES), lambda i: (i, 0, 0))],
        out_shape=[packed] * 3 + [jax.ShapeDtypeStruct((s // tm, 8, LANES), F32)],
        scratch_shapes=[pltpu.VMEM((8, LANES), F32)],
        compiler_params=_params(("arbitrary",)),
    )(z_all, z_all, bias)


def _attn_kernel(q_ref, k_ref, v_ref, fs_ref, o_ref, m_scr, acc_scr, *, tile):
    h = pl.program_id(0)
    i = pl.program_id(1)
    q = q_ref[...]
    reps = tile // LANES
    row = lax.broadcasted_iota(jnp.int32, (tile, tile), 0)
    col = lax.broadcasted_iota(jnp.int32, (tile, tile), 1)
    fs_q = fs_ref[i, pl.ds(h, 1), :]

    m_scr[...] = jnp.full(m_scr.shape, NEG_BIG, F32)
    acc_scr[...] = jnp.zeros_like(acc_scr)

    def run(tiles):
        offs = [pl.multiple_of(j * tile, tile) for j, _, _ in tiles]
        m_old = [m_scr[slot] for _, slot, _ in tiles]
        acc_old = [acc_scr[slot] for _, slot, _ in tiles]
        shift = [fs_q - fs_ref[j, pl.ds(h, 1), :] for j, _, _ in tiles]
        s = [_dot_nt(q, k_ref[pl.ds(off, tile), :]) for off in offs]
        s = [jnp.where(col <= row, x, NEG_BIG) if masked else x for x, (_, _, masked) in zip(s, tiles)]
        part = []
        for x in s:
            pm = x[:, 0:LANES]
            for n in range(1, reps):
                pm = jnp.maximum(pm, x[:, n * LANES:(n + 1) * LANES])
            part.append(pm)
        m_new = [jnp.maximum(mo, jnp.max(pm, axis=-1, keepdims=True) + sh)
                 for mo, pm, sh in zip(m_old, part, shift)]
        p = [jnp.exp2(x - pltpu.repeat(mn - sh, reps, axis=1)).astype(BF16) for x, mn, sh in zip(s, m_new, shift)]
        pv = [_dot(x, v_ref[pl.ds(off, tile), :]) for x, off in zip(p, offs)]
        for n, (_, slot, _) in enumerate(tiles):
            acc_scr[slot] = jnp.exp2(m_old[n] - m_new[n]) * acc_old[n] + pv[n]
            m_scr[slot] = m_new[n]

    def body(jj, carry):
        run([(2 * jj, 0, False), (2 * jj + 1, 1, False)])
        return carry

    lax.fori_loop(0, i // 2, body, 0)

    @pl.when(i % 2 == 1)
    def _():
        run([(i - 1, 0, False)])

    run([(i, 1, True)])
    m0, m1 = m_scr[0], m_scr[1]
    m = jnp.maximum(m0, m1)
    acc = jnp.exp2(m0 - m) * acc_scr[0] + jnp.exp2(m1 - m) * acc_scr[1]
    lane = lax.broadcasted_iota(jnp.int32, (1, LANES), 1)
    o = jnp.where(lane < HEAD_DIM, acc / acc[:, LANE_ONE:LANE_ONE + 1], 0.0)
    o_ref[...] = o.astype(o_ref.dtype)


def _attention(qp, kp, vp, fs):
    s, wide = qp.shape
    tile = min(ATT_TILE, s)
    return pl.pallas_call(
        functools.partial(_attn_kernel, tile=tile),
        grid=(N_HEADS, s // tile),
        in_specs=[
            pl.BlockSpec((tile, LANES), lambda h, i: (i, h)),
            pl.BlockSpec((s, LANES), lambda h, i: (0, h)),
            pl.BlockSpec((s, LANES), lambda h, i: (0, h)),
            pl.BlockSpec(fs.shape, lambda h, i: (0, 0, 0)),
        ],
        out_specs=pl.BlockSpec((tile, LANES), lambda h, i: (i, h)),
        out_shape=jax.ShapeDtypeStruct((s, wide), BF16),
        scratch_shapes=[pltpu.VMEM((2, tile, LANES), F32), pltpu.VMEM((2, tile, LANES), F32)],
        compiler_params=_params(("parallel", "arbitrary")),
    )(qp, kp, vp, fs)


def _lru_kernel(xb_ref, yb_ref, par_ref, cw_ref, ga_ref, gx_ref, o_ref,
                xpad_scr, a_scr, u_scr, h_scr, carry_scr):
    tm, w = xb_ref.shape

    @pl.when(pl.program_id(0) == 0)
    def _():
        xpad_scr[0:8, :] = jnp.zeros((8, w), F32)
        carry_scr[...] = jnp.zeros_like(carry_scr)

    xb = xb_ref[...]
    xpad_scr[8:8 + tm, :] = xb
    xc = xb * cw_ref[LRU_CONV - 1:LRU_CONV, :] + par_ref[0:1, :]
    for d in range(1, LRU_CONV):
        xc = xc + xpad_scr[8 - d:8 - d + tm, :] * cw_ref[LRU_CONV - 1 - d:LRU_CONV - d, :]
    xpad_scr[0:8, :] = xb[tm - 8:tm, :]

    xcb = xc.astype(BF16)
    r = _sigmoid(_dot(xcb, ga_ref[...]) + par_ref[1:2, :])
    gi = _sigmoid(_dot(xcb, gx_ref[...]) + par_ref[2:3, :])
    log_a = (-LRU_C) * r * _softplus(-par_ref[3:4, :])
    a = jnp.exp(log_a)
    u = jnp.sqrt(-jnp.tanh(log_a) * (a * a + 1.0)) * (gi * xc)

    r8 = lax.broadcasted_iota(jnp.int32, (tm, w), 0) % 8
    for d in (1, 2, 4):
        keep = r8 >= d
        a_prev = jnp.where(keep, pltpu.roll(a, d, axis=0), 1.0)
        u_prev = jnp.where(keep, pltpu.roll(u, d, axis=0), 0.0)
        u = a * u_prev + u
        a = a * a_prev
    a_scr[...] = a
    u_scr[...] = u

    def slab(k, carry):
        off = pl.multiple_of(k * 8, 8)
        h8 = a_scr[pl.ds(off, 8), :] * carry + u_scr[pl.ds(off, 8), :]
        h_scr[pl.ds(off, 8), :] = h8
        return jnp.broadcast_to(h8[7:8, :], (8, w))

    carry_scr[...] = lax.fori_loop(0, tm // 8, slab, carry_scr[...], unroll=4)

    yb = yb_ref[...]
    gelu = 0.5 * yb * (1.0 + jnp.tanh(0.7978845608028654 * (yb + 0.044715 * (yb * yb * yb))))
    o_ref[...] = (h_scr[...] * gelu).astype(o_ref.dtype)


def _lru(z_lru, par, conv_w, ga, gx):
    s = z_lru.shape[0]
    w = LRU_WIDTH
    tm = min(256, s)
    return pl.pallas_call(
        _lru_kernel,
        grid=(s // tm,),
        in_specs=[
            pl.BlockSpec((tm, w), lambda i: (i, COL_LRU // w)),
            pl.BlockSpec((tm, w), lambda i: (i, COL_LRU // w + 1)),
            pl.BlockSpec((8, w), lambda i: (0, 0)),
            pl.BlockSpec((8, w), lambda i: (0, 0)),
            pl.BlockSpec((w, w), lambda i: (0, 0)),
            pl.BlockSpec((w, w), lambda i: (0, 0)),
        ],
        out_specs=pl.BlockSpec((tm, w), lambda i: (i, 0)),
        out_shape=jax.ShapeDtypeStruct((s, w), BF16),
        scratch_shapes=[
            pltpu.VMEM((tm + 8, w), F32),
            pltpu.VMEM((tm, w), F32),
            pltpu.VMEM((tm, w), F32),
            pltpu.VMEM((tm, w), F32),
            pltpu.VMEM((8, w), F32),
        ],
        compiler_params=_params(("arbitrary",)),
    )(z_lru, z_lru, par, conv_w, ga, gx)


def _rwkv_prep_kernel(z_ref, mu_ref, par_ref, w2_ref, a2_ref, g2_ref, blk_ref,
                      r_ref, ld_ref, k_ref, v_ref, an_ref, b_ref, bonus_ref, g_ref, prev_scr):
    tm = z_ref.shape[0]
    wd = RWKV_WIDTH

    @pl.when(pl.program_id(0) == 0)
    def _():
        prev_scr[...] = jnp.zeros_like(prev_scr)

    z = z_ref[...]
    row = lax.broadcasted_iota(jnp.int32, z.shape, 0)
    z_prev = jnp.where(row == 0, prev_scr[0:1, :], pltpu.roll(z, 1, axis=0))
    prev_scr[...] = jnp.broadcast_to(z[tm - 1:tm, :], prev_scr.shape)
    zs = z + (z_prev - z) * mu_ref[0:1, :]

    r = zs[:, 0:wd]
    k = zs[:, wd:2 * wd]
    v = zs[:, 2 * wd:3 * wd]
    wl = zs[:, 3 * wd:3 * wd + LORA_PAD]
    al = zs[:, 3 * wd + LORA_PAD:3 * wd + 2 * LORA_PAD]
    gl = zs[:, 3 * wd + 2 * LORA_PAD:3 * wd + 3 * LORA_PAD]
    w0, a0, k_k, k_a, r_k = (par_ref[n:n + 1, :] for n in range(5))

    w = -_softplus(-(w0 + _dot_x3(jnp.tanh(wl), w2_ref[...]))) - 0.5
    a = _sigmoid(a0 + _dot_x3(al, a2_ref[...]))
    g = _dot_x3(_sigmoid(gl), g2_ref[...])

    blk = blk_ref[...]
    kk = k * k_k
    norm = jnp.sqrt(_dot_exact_rhs(kk * kk, blk))
    kk = kk / jnp.maximum(norm, 1e-12)
    kf = k * (1.0 + (a - 1.0) * k_a)

    r_ref[...] = r
    ld_ref[...] = -jnp.exp(w)
    k_ref[...] = kf
    v_ref[...] = v
    an_ref[...] = -kk
    b_ref[...] = kk * a
    bonus_ref[...] = _dot_exact_rhs(r * kf * r_k, blk) * v
    g_ref[...] = g


def _rwkv_prep(zc, mu, par, w2, a2, g2, blk):
    s = zc.shape[0]
    n = RWKV_COLS_PAD
    wd = RWKV_WIDTH
    tm = min(256, s)
    full = lambda shape: pl.BlockSpec(shape, lambda i: (0, 0))
    out = jax.ShapeDtypeStruct((s, wd), F32)
    return pl.pallas_call(
        _rwkv_prep_kernel,
        grid=(s // tm,),
        in_specs=[
            pl.BlockSpec((tm, n), lambda i: (i, COL_RWKV // n)),
            full((8, n)), full((8, wd)), full((LORA_PAD, wd)), full((LORA_PAD, wd)),
            full((LORA_PAD, wd)), full((wd, wd)),
        ],
        out_specs=[pl.BlockSpec((tm, wd), lambda i: (i, 0))] * 8,
        out_shape=[out] * 8,
        scratch_shapes=[pltpu.VMEM((8, n), F32)],
        compiler_params=_params(("arbitrary",)),
    )(zc, mu, par, w2, a2, g2, blk)


def _rwkv_scan_kernel(r_ref, ld_ref, k_ref, v_ref, an_ref, b_ref, y_ref, state_scr):
    t = r_ref.shape[0]

    @pl.when(pl.program_id(0) == 0)
    def _():
        state_scr[...] = jnp.zeros_like(state_scr)

    row3 = lax.broadcasted_iota(jnp.int32, (t, 3 * t), 0)
    col3 = lax.broadcasted_iota(jnp.int32, (t, 3 * t), 1)
    tril3 = jnp.where(col3 % t <= row3, 1.0, 0.0).astype(BF16)
    ld = ld_ref[...]
    c_incl = _dot(tril3, jnp.concatenate(_split3(ld), axis=0))
    p_incl = jnp.exp(c_incl)
    p_prev = jnp.exp(c_incl - ld)
    p_inv = jnp.exp(-c_incl)
    r_s = r_ref[...] * p_incl
    a_s = an_ref[...] * p_prev
    b_s = b_ref[...] * p_inv
    k_s = k_ref[...] * p_inv
    v_all = v_ref[...]

    gh = SCAN_GROUP_HEADS
    gw = gh * HEAD_DIM
    tg = gh * t
    lane = lax.broadcasted_iota(jnp.int32, (1, gw), 1)
    head_masks = [jnp.logical_and(lane >= n * HEAD_DIM, lane < (n + 1) * HEAD_DIM) for n in range(gh)]
    rowg = lax.broadcasted_iota(jnp.int32, (tg, tg), 0)
    colg = lax.broadcasted_iota(jnp.int32, (tg, tg), 1)
    same_head = (rowg // t) == (colg // t)
    strict = jnp.logical_and(same_head, (colg % t) < (rowg % t))
    incl = jnp.logical_and(same_head, (colg % t) <= (rowg % t))

    def stack(x):
        return jnp.concatenate([jnp.where(mask, x, 0.0) for mask in head_masks], axis=0)

    groups = range(N_HEADS // gh)
    sls = [slice(g * gw, (g + 1) * gw) for g in groups]
    states = [state_scr[g] for g in groups]
    cat0 = lambda xs: jnp.concatenate(xs, axis=0)
    cat0_s = lambda p, q: (cat0([p[0], q[0]]), cat0([p[1], q[1]]))

    a2_s = [_split2(stack(a_s[:, sl])) for sl in sls]
    r2_s = [_split2(stack(r_s[:, sl])) for sl in sls]
    v2 = [stack(v_all[:, sl]) for sl in sls]
    v2_s = [_split2(x) for x in v2]
    bk_s = [_split2(cat0([stack(b_s[:, sl]), stack(k_s[:, sl])])) for sl in sls]
    ar_s = [cat0_s(a2_s[g], r2_s[g]) for g in groups]
    m = [_mm3_nt(ar_s[g], bk_s[g]) for g in groups]
    from_state = [_mm3_nt(ar_s[g], _split2(states[g])) for g in groups]
    lt_k = [cat0([jnp.where(strict, m[g][0:tg, tg:2 * tg], 0.0),
                  jnp.where(incl, m[g][tg:2 * tg, tg:2 * tg], 0.0)]) for g in groups]
    from_v = [_mm3(_split2(lt_k[g]), v2_s[g]) for g in groups]

    sol = [from_state[g][0:tg] + from_v[g][0:tg] for g in groups]
    power = [jnp.where(strict, m[g][0:tg, 0:tg], 0.0) for g in groups]
    steps = max(1, (t - 1).bit_length())
    for n in range(steps):
        last = n == steps - 1
        for g in groups:
            rhs_n = sol[g] if last else jnp.concatenate([sol[g], power[g]], axis=1)
            prod = _mm3(_split2(power[g]), _split2(rhs_n))
            sol[g] = sol[g] + prod[:, 0:gw]
            if not last:
                power[g] = prod[:, gw:gw + tg]

    for g in groups:
        t_rb = jnp.where(incl, m[g][tg:2 * tg, 0:tg], 0.0)
        y2 = from_state[g][tg:2 * tg] + from_v[g][tg:2 * tg] + _mm3(_split2(t_rb), _split2(sol[g]))
        y = y2[0:t, :]
        for n in range(1, gh):
            y = y + y2[n * t:(n + 1) * t, :]
        y_ref[:, sls[g]] = y
    upd = [_mm3(_split2(cat0([sol[g], v2[g]]).T), bk_s[g]) for g in groups]
    for g in groups:
        state_scr[g] = (states[g] + upd[g]) * p_incl[t - 1:t, sls[g]]


def _rwkv_scan(r, ld, k, v, an, b):
    s, wd = r.shape
    t = min(RWKV_CHUNK, s)
    spec = pl.BlockSpec((t, wd), lambda i: (i, 0))
    return pl.pallas_call(
        _rwkv_scan_kernel,
        grid=(s // t,),
        in_specs=[spec] * 6,
        out_specs=spec,
        out_shape=jax.ShapeDtypeStruct((s, wd), F32),
        scratch_shapes=[pltpu.VMEM((N_HEADS // SCAN_GROUP_HEADS,) + (SCAN_GROUP_HEADS * HEAD_DIM,) * 2, F32)],
        compiler_params=_params(("arbitrary",)),
    )(r, ld, k, v, an, b)


def _merge_kernel(x_ref, oa_ref, ob_ref, y_ref, bonus_ref, g_ref, ga_ref, gb_ref, gc_ref,
                  par_ref, gn_ref, blk_ref, wpa_ref, wpb_ref, wpc_ref, wo_ref, o_ref, *, alpha):
    blk = blk_ref[...]
    y = y_ref[...]
    mean = _dot_exact_rhs(y, blk) * (1.0 / HEAD_DIM)
    yc = y - mean
    var = _dot_exact_rhs(yc * yc, blk) * (1.0 / HEAD_DIM)
    yn = yc * lax.rsqrt(var + RWKV_GN_EPS) * gn_ref[0:1, :] + gn_ref[1:2, :]
    oc = ((yn + bonus_ref[...]) * g_ref[...]).astype(BF16)

    merged = (ga_ref[...] * _dot(oa_ref[...], wpa_ref[...])
              + gb_ref[...] * _dot(ob_ref[...], wpb_ref[...])
              + gc_ref[...] * _dot(oc, wpc_ref[...]))
    out = _dot(merged.astype(BF16), wo_ref[...])
    z = alpha * x_ref[...] + par_ref[0:1, :] * out
    o_ref[...] = _ln(z) * par_ref[1:2, :] + par_ref[2:3, :]


def _merge(x, oa, ob, y, bonus, g, z_all, par, gn, blk, wpa, wpb, wpc, wo, alpha):
    s, d = x.shape
    wd = RWKV_WIDTH
    tm = min(256, s)
    rows = lambda width, col=0: pl.BlockSpec((tm, width), lambda i, col=col: (i, col))
    full = lambda shape: pl.BlockSpec(shape, lambda i: (0, 0))
    gate0 = COL_GATE // d
    return pl.pallas_call(
        functools.partial(_merge_kernel, alpha=alpha),
        grid=(s // tm,),
        in_specs=[
            rows(d), rows(oa.shape[1]), rows(LRU_WIDTH), rows(wd), rows(wd), rows(wd),
            rows(d, gate0), rows(d, gate0 + 1), rows(d, gate0 + 2),
            full((8, d)), full((8, wd)), full((wd, wd)),
            full(wpa.shape), full((LRU_WIDTH, d)), full((wd, d)), full((d, d)),
        ],
        out_specs=rows(d),
        out_shape=jax.ShapeDtypeStruct((s, d), F32),
        compiler_params=_params(("parallel",)),
    )(x, oa, ob, y, bonus, g, z_all, z_all, z_all, par, gn, blk, wpa, wpb, wpc, wo)


def _block_diag(w):
    nb, n, _ = w.shape
    eye = jnp.eye(nb, dtype=w.dtype)
    return (eye[:, None, :, None] * w[:, :, None, :]).reshape(nb * n, nb * n)


def kernel(x, c, ada_w, ada_b, ln_g, ln_b, ffn_up, ffn_down, w_in, fox_f_bias, lru_conv_w, lru_conv_b, lru_ga_w, lru_ga_b, lru_gx_w, lru_gx_b, lru_lambda, rwkv_mu, rwkv_w0, rwkv_w2, rwkv_a0, rwkv_a2, rwkv_g2, rwkv_k_k, rwkv_k_a, rwkv_r_k, rwkv_gn_w, rwkv_gn_b, w_proj_a, w_proj_b, w_proj_c, w_out):
    batch, s, d = x.shape
    assert batch == 1 and d == 1024
    depth = ada_w.shape[0]
    alpha = float((2 * depth) ** 0.25)
    wd = RWKV_WIDTH

    ada = _ada_all(c, ada_w, ada_b)
    blk = _block_diag(jnp.ones((N_HEADS, HEAD_DIM, HEAD_DIM), BF16))

    o_att = 3 * ATT_WIDTH
    o_lru = o_att + N_HEADS
    o_rwkv = o_lru + 2 * LRU_WIDTH
    o_gate = o_rwkv + 3 * wd + 64 + 64 + 128
    lora = 64

    xs = x[0]
    for l in range(depth):
        mods = [ada[l, n * d:(n + 1) * d] for n in range(9)]
        w_l = w_in[l]

        xs = _ffn(xs, _pack_rows(mods[0:3], d), _pack_rows([ln_g[l, 0], ln_b[l, 0]], d),
                  ffn_up[l, 0].astype(BF16), ffn_down[l, 0].astype(BF16), alpha)

        w_c = w_l[:, o_rwkv:o_gate]
        zpad = lambda n: jnp.zeros((d, n), F32)
        w_all = jnp.concatenate([
            w_l[:, 0:o_att],
            w_l[:, o_att:o_lru], zpad(COL_LRU - COL_FL - N_HEADS),
            w_l[:, o_lru:o_rwkv],
            w_c[:, 0:3 * wd],
            w_c[:, 3 * wd:3 * wd + lora], zpad(LORA_PAD - lora),
            w_c[:, 3 * wd + lora:3 * wd + 2 * lora], zpad(LORA_PAD - lora),
            w_c[:, 3 * wd + 2 * lora:], zpad(RWKV_COLS_PAD - 3 * wd - 3 * LORA_PAD),
            w_l[:, o_gate:]], axis=1).astype(BF16)
        assert w_all.shape[1] == N_IN_PAD
        z_all = _in_proj(xs, _pack_rows(mods[3:5], d), w_all)

        f_bias = _pack_rows([jnp.pad(fox_f_bias[l], (0, LANES - N_HEADS))], LANES)
        qp, kp, vp, fs = _fox_pack(z_all, f_bias)
        o_a = _attention(qp, kp, vp, fs)
        wpa = jnp.pad(w_proj_a[l].reshape(N_HEADS, HEAD_DIM, d), ((0, 0), (0, LANES - HEAD_DIM), (0, 0)))
        wpa = wpa.reshape(N_HEADS * LANES, d).astype(BF16)

        lru_par = _pack_rows([lru_conv_b[l], lru_ga_b[l], lru_gx_b[l], lru_lambda[l]], LRU_WIDTH)
        conv_w = jnp.pad(lru_conv_w[l], ((0, 8 - LRU_CONV), (0, 0)))
        o_b = _lru(z_all, lru_par, conv_w, _block_diag(lru_ga_w[l]).astype(BF16),
                   _block_diag(lru_gx_w[l]).astype(BF16))

        mu = rwkv_mu[l]
        zv = lambda n: jnp.zeros((n,), F32)
        mu_pad = jnp.concatenate([
            mu[0:3 * wd], mu[3 * wd:3 * wd + lora], zv(LORA_PAD - lora),
            mu[3 * wd + lora:3 * wd + 2 * lora], zv(LORA_PAD - lora),
            mu[3 * wd + 2 * lora:], zv(RWKV_COLS_PAD - 3 * wd - 3 * LORA_PAD)])
        rw_par = _pack_rows([rwkv_w0[l], rwkv_a0[l], rwkv_k_k[l], rwkv_k_a[l], rwkv_r_k[l].reshape(wd)], wd)
        pad_rows = lambda w: jnp.pad(w, ((0, LORA_PAD - w.shape[0]), (0, 0)))
        r, ld, kf, v, an, b, bonus, g = _rwkv_prep(
            z_all, _pack_rows([mu_pad], RWKV_COLS_PAD), rw_par,
            pad_rows(rwkv_w2[l]), pad_rows(rwkv_a2[l]), rwkv_g2[l], blk)
        y = _rwkv_scan(r, ld, kf, v, an, b)

        par = _pack_rows([mods[5], ln_g[l, 1], ln_b[l, 1]], d)
        gn = _pack_rows([rwkv_gn_w[l], rwkv_gn_b[l]], wd)
        xs = _merge(xs, o_a, o_b, y, bonus, g, z_all, par, gn, blk,
                    wpa, w_proj_b[l].astype(BF16), w_proj_c[l].astype(BF16),
                    w_out[l].astype(BF16), alpha)

        xs = _ffn(xs, _pack_rows(mods[6:9], d), _pack_rows([ln_g[l, 2], ln_b[l, 2]], d),
                  ffn_up[l, 1].astype(BF16), ffn_down[l, 1].astype(BF16), alpha)

    return xs[None]
```

```python
import functools

import jax
import jax.numpy as jnp
from jax import lax
from jax.experimental import pallas as pl
from jax.experimental.pallas import tpu as pltpu

F32 = jnp.float32
BF16 = jnp.bfloat16
HI = lax.Precision.HIGHEST

LANES = 128
HEAD_DIM = 64
N_HEADS = 8
ATT_WIDTH = 512
LRU_WIDTH = 1024
LRU_BLOCK = 64
LRU_CONV = 4
LRU_C = 8.0
RWKV_WIDTH = 512
LORA_PAD = 128
RWKV_COLS_PAD = 2048
RWKV_GN_EPS = 64e-5
LN_EPS = 1e-5
NEG_BIG = -1e30
VMEM_LIMIT = 56 * 1024 * 1024

RWKV_CHUNK = 64
SCAN_GROUP_HEADS = 2
ATT_TILE = 512
ATT_ROW_BLOCK = 64
LOG2E = 1.4426950408889634

COL_QKV = 0
COL_FL = 1536
COL_LRU = 2048
COL_RWKV = 4096
COL_GATE = 6144
N_IN_PAD = 9216

LANE_ONE = HEAD_DIM
LANE_F = HEAD_DIM + 3


def _dot(a, b, prec=None):
    return jnp.dot(a, b, preferred_element_type=F32, precision=prec)


def _dot_nt(a, b, prec=None):
    return lax.dot_general(a, b, (((1,), (1,)), ((), ())), preferred_element_type=F32, precision=prec)


def _split2(x):
    hi = x.astype(BF16)
    return hi, (x - hi.astype(F32)).astype(BF16)


def _split3(x):
    hi = x.astype(BF16)
    r = x - hi.astype(F32)
    mid = r.astype(BF16)
    return hi, mid, (r - mid.astype(F32)).astype(BF16)


def _mm3(a, b):
    cross = _dot(jnp.concatenate([a[0], a[1]], axis=1), jnp.concatenate([b[1], b[0]], axis=0))
    return _dot(a[0], b[0]) + cross


def _mm3_nt(a, b):
    cross = _dot_nt(jnp.concatenate([a[0], a[1]], axis=1), jnp.concatenate([b[1], b[0]], axis=1))
    return _dot_nt(a[0], b[0]) + cross


def _dot_x3(a, b):
    return _mm3(_split2(a), _split2(b))


def _dot_exact_lhs(sel, x):
    hi, mid, lo = _split3(x)
    return _dot(sel, hi) + (_dot(sel, mid) + _dot(sel, lo))


def _dot_exact_rhs(x, sel):
    hi, mid, lo = _split3(x)
    return _dot(hi, sel) + (_dot(mid, sel) + _dot(lo, sel))


def _head_sums(x, blk):
    hi, lo = _split2(x)
    return _dot(hi, blk) + _dot(lo, blk)


def _ln(x):
    mu = jnp.mean(x, axis=-1, keepdims=True)
    xc = x - mu
    var = jnp.mean(xc * xc, axis=-1, keepdims=True)
    return xc * lax.rsqrt(var + LN_EPS)


def _softplus(x):
    return jnp.maximum(x, 0.0) + jnp.log1p(jnp.exp(-jnp.abs(x)))


def _sigmoid(x):
    return 1.0 / (1.0 + jnp.exp(-x))


def _params(sem):
    return pltpu.CompilerParams(dimension_semantics=sem, vmem_limit_bytes=VMEM_LIMIT)


def _pack_rows(rows, width):
    rows = [r.reshape(1, width).astype(F32) for r in rows]
    pad = jnp.zeros((8 - len(rows), width), F32)
    return jnp.concatenate(rows + [pad], axis=0)


def _ada_kernel(c_ref, w_ref, b_ref, o_ref):
    c = c_ref[...]
    ca = c * _sigmoid(c)
    o_ref[0] = _dot(ca, w_ref[0], HI) + b_ref[0]


def _ada_all(c, ada_w, ada_b):
    n_layers, d, n = ada_w.shape
    tn = 1152
    c8 = jnp.broadcast_to(c.astype(F32), (8, d))
    out = pl.pallas_call(
        _ada_kernel,
        grid=(n_layers, n // tn),
        in_specs=[
            pl.BlockSpec((8, d), lambda l, j: (0, 0)),
            pl.BlockSpec((1, d, tn), lambda l, j: (l, 0, j)),
            pl.BlockSpec((1, 1, tn), lambda l, j: (l, 0, j)),
        ],
        out_specs=pl.BlockSpec((1, 8, tn), lambda l, j: (l, 0, j)),
        out_shape=jax.ShapeDtypeStruct((n_layers, 8, n), F32),
        compiler_params=_params(("arbitrary", "arbitrary")),
    )(c8, ada_w, ada_b.reshape(n_layers, 1, n))
    return out[:, 0, :]


def _ffn_kernel(x_ref, mod_ref, lnp_ref, wup_ref, wd_ref, o_ref, *, alpha, tf):
    d_ff = wd_ref.shape[0]
    n_chunks = d_ff // tf
    x = x_ref[...]
    h = (_ln(x) * (1.0 + mod_ref[1:2, :]) + mod_ref[0:1, :]).astype(BF16)

    def up(c):
        u = _dot(h, wup_ref[:, c * tf:(c + 1) * tf])
        g = _dot(h, wup_ref[:, d_ff + c * tf:d_ff + (c + 1) * tf])
        return u, g

    acc = None
    nxt = up(0)
    for c in range(n_chunks):
        u, g = nxt
        if c + 1 < n_chunks:
            nxt = up(c + 1)
        act = (u * _sigmoid(u) * g).astype(BF16)
        down = _dot(act, wd_ref[c * tf:(c + 1) * tf, :])
        acc = down if acc is None else acc + down

    z = alpha * x + (0.5 * mod_ref[2:3, :]) * acc
    o_ref[...] = _ln(z) * lnp_ref[0:1, :] + lnp_ref[1:2, :]


def _ffn(x, mod, lnp, w_up, w_down, alpha):
    s, d = x.shape
    d_ff = w_down.shape[0]
    tm = min(512, s)
    return pl.pallas_call(
        functools.partial(_ffn_kernel, alpha=alpha, tf=256),
        grid=(s // tm,),
        in_specs=[
            pl.BlockSpec((tm, d), lambda i: (i, 0)),
            pl.BlockSpec((8, d), lambda i: (0, 0)),
            pl.BlockSpec((8, d), lambda i: (0, 0)),
            pl.BlockSpec((d, 2 * d_ff), lambda i: (0, 0)),
            pl.BlockSpec((d_ff, d), lambda i: (0, 0)),
        ],
        out_specs=pl.BlockSpec((tm, d), lambda i: (i, 0)),
        out_shape=jax.ShapeDtypeStruct((s, d), F32),
        compiler_params=_params(("parallel",)),
    )(x, mod, lnp, w_up, w_down)


def _in_proj_kernel(x_ref, mod_ref, w_ref, o_ref, h_scr, *, first_gate_tile):
    j = pl.program_id(1)

    @pl.when(j == 0)
    def _():
        h = _ln(x_ref[...]) * (1.0 + mod_ref[1:2, :]) + mod_ref[0:1, :]
        h_scr[...] = h.astype(BF16)

    z = _dot(h_scr[...], w_ref[...])

    @pl.when(j < first_gate_tile)
    def _():
        o_ref[...] = z

    @pl.when(j >= first_gate_tile)
    def _():
        o_ref[...] = _sigmoid(z)


def _in_proj(x, mod, w):
    s, d = x.shape
    n = w.shape[1]
    tm = min(2048, s)
    tn = 512
    return pl.pallas_call(
        functools.partial(_in_proj_kernel, first_gate_tile=COL_GATE // tn),
        grid=(s // tm, n // tn),
        in_specs=[
            pl.BlockSpec((tm, d), lambda i, j: (i, 0)),
            pl.BlockSpec((8, d), lambda i, j: (0, 0)),
            pl.BlockSpec((d, tn), lambda i, j: (0, j)),
        ],
        out_specs=pl.BlockSpec((tm, tn), lambda i, j: (i, j)),
        out_shape=jax.ShapeDtypeStruct((s, n), F32),
        scratch_shapes=[pltpu.VMEM((tm, d), BF16)],
        compiler_params=_params(("parallel", "arbitrary")),
    )(x, mod, w)


def _fox_pack_kernel(qkv_ref, fl_ref, bias_ref, qp_ref, kp_ref, vp_ref, fs_ref, carry):
    tm = fl_ref.shape[0]

    @pl.when(pl.program_id(0) == 0)
    def _():
        carry[...] = jnp.zeros_like(carry)

    log_f = -_softplus(-(fl_ref[...] + bias_ref[0:1, :]))
    row = lax.broadcasted_iota(jnp.int32, (tm, tm), 0)
    col = lax.broadcasted_iota(jnp.int32, (tm, tm), 1)
    tril = jnp.where(col <= row, 1.0, 0.0).astype(BF16)
    f_loc = _dot_exact_lhs(tril, log_f * LOG2E)

    start = carry[...]
    r8 = lax.broadcasted_iota(jnp.int32, (8, LANES), 0)
    c8 = lax.broadcasted_iota(jnp.int32, (8, LANES), 1)
    diag = jnp.where(r8 == c8, start, 0.0)
    fs_ref[0] = _dot_exact_rhs(diag, jnp.ones((LANES, LANES), BF16))
    carry[...] = start + f_loc[tm - 1:tm, :]

    lane = lax.broadcasted_iota(jnp.int32, (1, LANES), 1)
    head_lanes = lane < HEAD_DIM
    ones3 = jnp.logical_and(lane >= LANE_ONE, lane < LANE_ONE + 3).astype(F32)
    onesf = jnp.logical_and(lane >= LANE_F, lane < LANE_F + 3).astype(F32)
    one1 = (lane == LANE_ONE).astype(F32)
    for h in range(N_HEADS):
        f = jnp.broadcast_to(f_loc[:, h:h + 1], (tm, LANES))
        pieces = [p.astype(F32) for p in _split3(f)]
        q_extra = ones3
        k_extra = onesf
        for n, piece in enumerate(pieces):
            q_extra = q_extra + jnp.where(lane == LANE_F + n, piece, 0.0)
            k_extra = k_extra - jnp.where(lane == LANE_ONE + n, piece, 0.0)
        pair = h // 2
        heads = []
        for base in (0, ATT_WIDTH, 2 * ATT_WIDTH):
            xh = qkv_ref[:, base + pair * LANES:base + (pair + 1) * LANES]
            heads.append(pltpu.roll(xh, HEAD_DIM, axis=1) if h % 2 else xh)
        sl = slice(h * LANES, (h + 1) * LANES)
        qp_ref[:, sl] = jnp.where(head_lanes, heads[0] * (LOG2E * HEAD_DIM ** -0.5), q_extra).astype(BF16)
        kp_ref[:, sl] = jnp.where(head_lanes, heads[1], k_extra).astype(BF16)
        vp_ref[:, sl] = jnp.where(head_lanes, heads[2], one1).astype(BF16)


def _fox_pack(z_all, bias):
    s = z_all.shape[0]
    tm = min(ATT_TILE, s)
    wide = N_HEADS * LANES
    packed = jax.ShapeDtypeStruct((s, wide), BF16)
    return pl.pallas_call(
        _fox_pack_kernel,
        grid=(s // tm,),
        in_specs=[
            pl.BlockSpec((tm, 3 * ATT_WIDTH), lambda i: (i, COL_QKV // (3 * ATT_WIDTH))),
            pl.BlockSpec((tm, LANES), lambda i: (i, COL_FL // LANES)),
            pl.BlockSpec((8, LANES), lambda i: (0, 0)),
        ],
        out_specs=[pl.BlockSpec((tm, wide), lambda i: (i, 0))] * 3
        + [pl.BlockSpec((1, 8, LANES), lambda i: (i, 0, 0))],
        out_shape=[packed] * 3 + [jax.ShapeDtypeStruct((s // tm, 8, LANES), F32)],
        scratch_shapes=[pltpu.VMEM((8, LANES), F32)],
        compiler_params=_params(("arbitrary",)),
    )(z_all, z_all, bias)


def _attn_kernel(q_ref, k_ref, v_ref, fs_ref, o_ref, m_scr, acc_scr, s_a, s_b, p_scr, *, tile):
    h = pl.program_id(0)
    i = pl.program_id(1)
    q = q_ref[...]
    reps = tile // LANES
    row = lax.broadcasted_iota(jnp.int32, (ATT_ROW_BLOCK, tile), 0)
    col = lax.broadcasted_iota(jnp.int32, (ATT_ROW_BLOCK, tile), 1)
    fs_q = fs_ref[i, pl.ds(h, 1), :]

    m_scr[...] = jnp.full(m_scr.shape, NEG_BIG, F32)
    acc_scr[...] = jnp.zeros_like(acc_scr)

    def scores(j, dst):
        off = pl.multiple_of(j * tile, tile)
        dst[...] = _dot_nt(q, k_ref[pl.ds(off, tile), :])

    def phase(j, src, dst, masked):
        if dst is not None:
            scores(j + 1, dst)
        off = pl.multiple_of(j * tile, tile)
        shift = fs_q - fs_ref[j, pl.ds(h, 1), :]
        blocks = [slice(rb * ATT_ROW_BLOCK, (rb + 1) * ATT_ROW_BLOCK) for rb in range(tile // ATT_ROW_BLOCK)]

        def load(rb):
            s = src[blocks[rb], :]
            return jnp.where(col <= row + rb * ATT_ROW_BLOCK, s, NEG_BIG) if masked else s

        for rb, rs in enumerate(blocks):
            s = load(rb)
            part = s[:, 0:LANES]
            for n in range(1, reps):
                part = jnp.maximum(part, s[:, n * LANES:(n + 1) * LANES])
            m_old = m_scr[rs, :]
            m_new = jnp.maximum(m_old, jnp.max(part, axis=-1, keepdims=True) + shift)
            acc_scr[rs, :] = jnp.exp2(m_old - m_new) * acc_scr[rs, :]
            m_scr[rs, :] = m_new
        for rb, rs in enumerate(blocks):
            p_scr[rs, :] = jnp.exp2(load(rb) - pltpu.repeat(m_scr[rs, :] - shift, reps, axis=1)).astype(BF16)
        acc_scr[...] += _dot(p_scr[...], v_ref[pl.ds(off, tile), :])

    scores(0, s_a)

    def body(jj, carry):
        phase(2 * jj, s_a, s_b, False)
        phase(2 * jj + 1, s_b, s_a, False)
        return carry

    lax.fori_loop(0, i // 2, body, 0)

    @pl.when(i % 2 == 0)
    def _():
        phase(i, s_a, None, True)

    @pl.when(i % 2 == 1)
    def _():
        phase(i - 1, s_a, s_b, False)
        phase(i, s_b, None, True)

    acc = acc_scr[...]
    lane = lax.broadcasted_iota(jnp.int32, (1, LANES), 1)
    o = jnp.where(lane < HEAD_DIM, acc / acc[:, LANE_ONE:LANE_ONE + 1], 0.0)
    o_ref[...] = o.astype(o_ref.dtype)


def _attention(qp, kp, vp, fs):
    s, wide = qp.shape
    tile = min(ATT_TILE, s)
    return pl.pallas_call(
        functools.partial(_attn_kernel, tile=tile),
        grid=(N_HEADS, s // tile),
        in_specs=[
            pl.BlockSpec((tile, LANES), lambda h, i: (i, h)),
            pl.BlockSpec((s, LANES), lambda h, i: (0, h)),
            pl.BlockSpec((s, LANES), lambda h, i: (0, h)),
            pl.BlockSpec(fs.shape, lambda h, i: (0, 0, 0)),
        ],
        out_specs=pl.BlockSpec((tile, LANES), lambda h, i: (i, h)),
        out_shape=jax.ShapeDtypeStruct((s, wide), BF16),
        scratch_shapes=[pltpu.VMEM((tile, LANES), F32), pltpu.VMEM((tile, LANES), F32),
                        pltpu.VMEM((tile, tile), F32), pltpu.VMEM((tile, tile), F32),
                        pltpu.VMEM((tile, tile), BF16)],
        compiler_params=_params(("parallel", "arbitrary")),
    )(qp, kp, vp, fs)


def _lru_kernel(xb_ref, yb_ref, par_ref, cw_ref, ga_ref, gx_ref, o_ref,
                xpad_scr, a_scr, u_scr, h_scr, carry_scr):
    tm, w = xb_ref.shape

    @pl.when(pl.program_id(0) == 0)
    def _():
        xpad_scr[0:8, :] = jnp.zeros((8, w), F32)
        carry_scr[...] = jnp.zeros_like(carry_scr)

    xb = xb_ref[...]
    xpad_scr[8:8 + tm, :] = xb
    xc = xb * cw_ref[LRU_CONV - 1:LRU_CONV, :] + par_ref[0:1, :]
    for d in range(1, LRU_CONV):
        xc = xc + xpad_scr[8 - d:8 - d + tm, :] * cw_ref[LRU_CONV - 1 - d:LRU_CONV - d, :]
    xpad_scr[0:8, :] = xb[tm - 8:tm, :]

    xcb = xc.astype(BF16)
    r = _sigmoid(_dot(xcb, ga_ref[...]) + par_ref[1:2, :])
    gi = _sigmoid(_dot(xcb, gx_ref[...]) + par_ref[2:3, :])
    log_a = (-LRU_C) * r * _softplus(-par_ref[3:4, :])
    a = jnp.exp(log_a)
    u = jnp.sqrt(-jnp.tanh(log_a) * (a * a + 1.0)) * (gi * xc)

    r8 = lax.broadcasted_iota(jnp.int32, (tm, w), 0) % 8
    for d in (1, 2, 4):
        keep = r8 >= d
        a_prev = jnp.where(keep, pltpu.roll(a, d, axis=0), 1.0)
        u_prev = jnp.where(keep, pltpu.roll(u, d, axis=0), 0.0)
        u = a * u_prev + u
        a = a * a_prev
    a_scr[...] = a
    u_scr[...] = u

    def slab(k, carry):
        off = pl.multiple_of(k * 8, 8)
        h8 = a_scr[pl.ds(off, 8), :] * carry + u_scr[pl.ds(off, 8), :]
        h_scr[pl.ds(off, 8), :] = h8
        return jnp.broadcast_to(h8[7:8, :], (8, w))

    carry_scr[...] = lax.fori_loop(0, tm // 8, slab, carry_scr[...], unroll=4)

    yb = yb_ref[...]
    gelu = 0.5 * yb * (1.0 + jnp.tanh(0.7978845608028654 * (yb + 0.044715 * (yb * yb * yb))))
    o_ref[...] = (h_scr[...] * gelu).astype(o_ref.dtype)


def _lru(z_lru, par, conv_w, ga, gx):
    s = z_lru.shape[0]
    w = LRU_WIDTH
    tm = min(256, s)
    return pl.pallas_call(
        _lru_kernel,
        grid=(s // tm,),
        in_specs=[
            pl.BlockSpec((tm, w), lambda i: (i, COL_LRU // w)),
            pl.BlockSpec((tm, w), lambda i: (i, COL_LRU // w + 1)),
            pl.BlockSpec((8, w), lambda i: (0, 0)),
            pl.BlockSpec((8, w), lambda i: (0, 0)),
            pl.BlockSpec((w, w), lambda i: (0, 0)),
            pl.BlockSpec((w, w), lambda i: (0, 0)),
        ],
        out_specs=pl.BlockSpec((tm, w), lambda i: (i, 0)),
        out_shape=jax.ShapeDtypeStruct((s, w), BF16),
        scratch_shapes=[
            pltpu.VMEM((tm + 8, w), F32),
            pltpu.VMEM((tm, w), F32),
            pltpu.VMEM((tm, w), F32),
            pltpu.VMEM((tm, w), F32),
            pltpu.VMEM((8, w), F32),
        ],
        compiler_params=_params(("arbitrary",)),
    )(z_lru, z_lru, par, conv_w, ga, gx)


def _rwkv_prep_kernel(z_ref, mu_ref, par_ref, w2_ref, a2_ref, g2_ref, blk_ref,
                      r_ref, ld_ref, k_ref, v_ref, an_ref, b_ref, bonus_ref, g_ref, prev_scr):
    tm = z_ref.shape[0]
    wd = RWKV_WIDTH

    @pl.when(pl.program_id(0) == 0)
    def _():
        prev_scr[...] = jnp.zeros_like(prev_scr)

    z = z_ref[...]
    row = lax.broadcasted_iota(jnp.int32, z.shape, 0)
    z_prev = jnp.where(row == 0, prev_scr[0:1, :], pltpu.roll(z, 1, axis=0))
    prev_scr[...] = jnp.broadcast_to(z[tm - 1:tm, :], prev_scr.shape)
    zs = z + (z_prev - z) * mu_ref[0:1, :]

    r = zs[:, 0:wd]
    k = zs[:, wd:2 * wd]
    v = zs[:, 2 * wd:3 * wd]
    wl = zs[:, 3 * wd:3 * wd + LORA_PAD]
    al = zs[:, 3 * wd + LORA_PAD:3 * wd + 2 * LORA_PAD]
    gl = zs[:, 3 * wd + 2 * LORA_PAD:3 * wd + 3 * LORA_PAD]
    w0, a0, k_k, k_a, r_k = (par_ref[n:n + 1, :] for n in range(5))

    w = -_softplus(-(w0 + _dot_x3(jnp.tanh(wl), w2_ref[...]))) - 0.5
    a = _sigmoid(a0 + _dot_x3(al, a2_ref[...]))
    g = _dot_x3(_sigmoid(gl), g2_ref[...])

    blk = blk_ref[...]
    kk = k * k_k
    norm = jnp.sqrt(_head_sums(kk * kk, blk))
    kk = kk / jnp.maximum(norm, 1e-12)
    kf = k * (1.0 + (a - 1.0) * k_a)

    r_ref[...] = r
    ld_ref[...] = -jnp.exp(w)
    k_ref[...] = kf
    v_ref[...] = v
    an_ref[...] = -kk
    b_ref[...] = kk * a
    bonus_ref[...] = _head_sums(r * kf * r_k, blk) * v
    g_ref[...] = g


def _rwkv_prep(zc, mu, par, w2, a2, g2, blk):
    s = zc.shape[0]
    n = RWKV_COLS_PAD
    wd = RWKV_WIDTH
    tm = min(256, s)
    full = lambda shape: pl.BlockSpec(shape, lambda i: (0, 0))
    out = jax.ShapeDtypeStruct((s, wd), F32)
    return pl.pallas_call(
        _rwkv_prep_kernel,
        grid=(s // tm,),
        in_specs=[
            pl.BlockSpec((tm, n), lambda i: (i, COL_RWKV // n)),
            full((8, n)), full((8, wd)), full((LORA_PAD, wd)), full((LORA_PAD, wd)),
            full((LORA_PAD, wd)), full((wd, wd)),
        ],
        out_specs=[pl.BlockSpec((tm, wd), lambda i: (i, 0))] * 8,
        out_shape=[out] * 8,
        scratch_shapes=[pltpu.VMEM((8, n), F32)],
        compiler_params=_params(("arbitrary",)),
    )(zc, mu, par, w2, a2, g2, blk)


def _rwkv_scan_kernel(r_ref, ld_ref, k_ref, v_ref, an_ref, b_ref, y_ref, state_scr):
    t = r_ref.shape[0]

    @pl.when(pl.program_id(0) == 0)
    def _():
        state_scr[...] = jnp.zeros_like(state_scr)

    row3 = lax.broadcasted_iota(jnp.int32, (t, 3 * t), 0)
    col3 = lax.broadcasted_iota(jnp.int32, (t, 3 * t), 1)
    tril3 = jnp.where(col3 % t <= row3, 1.0, 0.0).astype(BF16)
    ld = ld_ref[...]
    c_incl = _dot(tril3, jnp.concatenate(_split3(ld), axis=0))
    p_incl = jnp.exp(c_incl)
    p_prev = jnp.exp(c_incl - ld)
    p_inv = jnp.exp(-c_incl)
    r_s = r_ref[...] * p_incl
    a_s = an_ref[...] * p_prev
    b_s = b_ref[...] * p_inv
    k_s = k_ref[...] * p_inv
    v_all = v_ref[...]

    gh = SCAN_GROUP_HEADS
    gw = gh * HEAD_DIM
    tg = gh * t
    lane = lax.broadcasted_iota(jnp.int32, (1, gw), 1)
    head_masks = [jnp.logical_and(lane >= n * HEAD_DIM, lane < (n + 1) * HEAD_DIM) for n in range(gh)]
    rowg = lax.broadcasted_iota(jnp.int32, (tg, tg), 0)
    colg = lax.broadcasted_iota(jnp.int32, (tg, tg), 1)
    same_head = (rowg // t) == (colg // t)
    strict = jnp.logical_and(same_head, (colg % t) < (rowg % t))
    incl = jnp.logical_and(same_head, (colg % t) <= (rowg % t))

    def stack(x):
        return jnp.concatenate([jnp.where(mask, x, 0.0) for mask in head_masks], axis=0)

    groups = range(N_HEADS // gh)
    sls = [slice(g * gw, (g + 1) * gw) for g in groups]
    states = [state_scr[g] for g in groups]
    cat0 = lambda xs: jnp.concatenate(xs, axis=0)
    cat0_s = lambda p, q: (cat0([p[0], q[0]]), cat0([p[1], q[1]]))

    a2_s = [_split2(stack(a_s[:, sl])) for sl in sls]
    r2_s = [_split2(stack(r_s[:, sl])) for sl in sls]
    v2 = [stack(v_all[:, sl]) for sl in sls]
    v2_s = [_split2(x) for x in v2]
    bk_s = [_split2(cat0([stack(b_s[:, sl]), stack(k_s[:, sl])])) for sl in sls]
    ar_s = [cat0_s(a2_s[g], r2_s[g]) for g in groups]
    m = [_mm3_nt(ar_s[g], bk_s[g]) for g in groups]
    from_state = [_mm3_nt(ar_s[g], _split2(states[g])) for g in groups]
    lt_k = [cat0([jnp.where(strict, m[g][0:tg, tg:2 * tg], 0.0),
                  jnp.where(incl, m[g][tg:2 * tg, tg:2 * tg], 0.0)]) for g in groups]
    from_v = [_mm3(_split2(lt_k[g]), v2_s[g]) for g in groups]

    sol = [from_state[g][0:tg] + from_v[g][0:tg] for g in groups]
    power = [jnp.where(strict, m[g][0:tg, 0:tg], 0.0) for g in groups]
    steps = max(1, (t - 1).bit_length())
    for n in range(steps):
        last = n == steps - 1
        for g in groups:
            rhs_n = sol[g] if last else jnp.concatenate([sol[g], power[g]], axis=1)
            prod = _mm3(_split2(power[g]), _split2(rhs_n))
            sol[g] = sol[g] + prod[:, 0:gw]
            if not last:
                power[g] = prod[:, gw:gw + tg]

    for g in groups:
        t_rb = jnp.where(incl, m[g][tg:2 * tg, 0:tg], 0.0)
        y2 = from_state[g][tg:2 * tg] + from_v[g][tg:2 * tg] + _mm3(_split2(t_rb), _split2(sol[g]))
        y = y2[0:t, :]
        for n in range(1, gh):
            y = y + y2[n * t:(n + 1) * t, :]
        y_ref[:, sls[g]] = y
    upd = [_mm3(_split2(cat0([sol[g], v2[g]]).T), bk_s[g]) for g in groups]
    for g in groups:
        state_scr[g] = (states[g] + upd[g]) * p_incl[t - 1:t, sls[g]]


def _rwkv_scan(r, ld, k, v, an, b):
    s, wd = r.shape
    t = min(RWKV_CHUNK, s)
    spec = pl.BlockSpec((t, wd), lambda i: (i, 0))
    return pl.pallas_call(
        _rwkv_scan_kernel,
        grid=(s // t,),
        in_specs=[spec] * 6,
        out_specs=spec,
        out_shape=jax.ShapeDtypeStruct((s, wd), F32),
        scratch_shapes=[pltpu.VMEM((N_HEADS // SCAN_GROUP_HEADS,) + (SCAN_GROUP_HEADS * HEAD_DIM,) * 2, F32)],
        compiler_params=_params(("arbitrary",)),
    )(r, ld, k, v, an, b)


def _merge_kernel(x_ref, oa_ref, ob_ref, y_ref, bonus_ref, g_ref, ga_ref, gb_ref, gc_ref,
                  par_ref, gn_ref, blk_ref, wpa_ref, wpb_ref, wpc_ref, wo_ref, o_ref, *, alpha):
    blk = blk_ref[...]
    y = y_ref[...]
    mean = _dot_exact_rhs(y, blk) * (1.0 / HEAD_DIM)
    yc = y - mean
    var = _head_sums(yc * yc, blk) * (1.0 / HEAD_DIM)
    yn = yc * lax.rsqrt(var + RWKV_GN_EPS) * gn_ref[0:1, :] + gn_ref[1:2, :]
    oc = ((yn + bonus_ref[...]) * g_ref[...]).astype(BF16)

    merged = (ga_ref[...] * _dot(oa_ref[...], wpa_ref[...])
              + gb_ref[...] * _dot(ob_ref[...], wpb_ref[...])
              + gc_ref[...] * _dot(oc, wpc_ref[...]))
    out = _dot(merged.astype(BF16), wo_ref[...])
    z = alpha * x_ref[...] + par_ref[0:1, :] * out
    o_ref[...] = _ln(z) * par_ref[1:2, :] + par_ref[2:3, :]


def _merge(x, oa, ob, y, bonus, g, z_all, par, gn, blk, wpa, wpb, wpc, wo, alpha):
    s, d = x.shape
    wd = RWKV_WIDTH
    tm = min(256, s)
    rows = lambda width, col=0: pl.BlockSpec((tm, width), lambda i, col=col: (i, col))
    full = lambda shape: pl.BlockSpec(shape, lambda i: (0, 0))
    gate0 = COL_GATE // d
    return pl.pallas_call(
        functools.partial(_merge_kernel, alpha=alpha),
        grid=(s // tm,),
        in_specs=[
            rows(d), rows(oa.shape[1]), rows(LRU_WIDTH), rows(wd), rows(wd), rows(wd),
            rows(d, gate0), rows(d, gate0 + 1), rows(d, gate0 + 2),
            full((8, d)), full((8, wd)), full((wd, wd)),
            full(wpa.shape), full((LRU_WIDTH, d)), full((wd, d)), full((d, d)),
        ],
        out_specs=rows(d),
        out_shape=jax.ShapeDtypeStruct((s, d), F32),
        compiler_params=_params(("parallel",)),
    )(x, oa, ob, y, bonus, g, z_all, z_all, z_all, par, gn, blk, wpa, wpb, wpc, wo)


def _block_diag(w):
    nb, n, _ = w.shape
    eye = jnp.eye(nb, dtype=w.dtype)
    return (eye[:, None, :, None] * w[:, :, None, :]).reshape(nb * n, nb * n)


def kernel(x, c, ada_w, ada_b, ln_g, ln_b, ffn_up, ffn_down, w_in, fox_f_bias, lru_conv_w, lru_conv_b, lru_ga_w, lru_ga_b, lru_gx_w, lru_gx_b, lru_lambda, rwkv_mu, rwkv_w0, rwkv_w2, rwkv_a0, rwkv_a2, rwkv_g2, rwkv_k_k, rwkv_k_a, rwkv_r_k, rwkv_gn_w, rwkv_gn_b, w_proj_a, w_proj_b, w_proj_c, w_out):
    batch, s, d = x.shape
    assert batch == 1 and d == 1024
    depth = ada_w.shape[0]
    alpha = float((2 * depth) ** 0.25)
    wd = RWKV_WIDTH

    ada = _ada_all(c, ada_w, ada_b)
    blk = _block_diag(jnp.ones((N_HEADS, HEAD_DIM, HEAD_DIM), BF16))

    o_att = 3 * ATT_WIDTH
    o_lru = o_att + N_HEADS
    o_rwkv = o_lru + 2 * LRU_WIDTH
    o_gate = o_rwkv + 3 * wd + 64 + 64 + 128
    lora = 64

    xs = x[0]
    for l in range(depth):
        mods = [ada[l, n * d:(n + 1) * d] for n in range(9)]
        w_l = w_in[l]

        xs = _ffn(xs, _pack_rows(mods[0:3], d), _pack_rows([ln_g[l, 0], ln_b[l, 0]], d),
                  ffn_up[l, 0].astype(BF16), ffn_down[l, 0].astype(BF16), alpha)

        w_c = w_l[:, o_rwkv:o_gate]
        zpad = lambda n: jnp.zeros((d, n), F32)
        w_all = jnp.concatenate([
            w_l[:, 0:o_att],
            w_l[:, o_att:o_lru], zpad(COL_LRU - COL_FL - N_HEADS),
            w_l[:, o_lru:o_rwkv],
            w_c[:, 0:3 * wd],
            w_c[:, 3 * wd:3 * wd + lora], zpad(LORA_PAD - lora),
            w_c[:, 3 * wd + lora:3 * wd + 2 * lora], zpad(LORA_PAD - lora),
            w_c[:, 3 * wd + 2 * lora:], zpad(RWKV_COLS_PAD - 3 * wd - 3 * LORA_PAD),
            w_l[:, o_gate:]], axis=1).astype(BF16)
        assert w_all.shape[1] == N_IN_PAD
        z_all = _in_proj(xs, _pack_rows(mods[3:5], d), w_all)

        f_bias = _pack_rows([jnp.pad(fox_f_bias[l], (0, LANES - N_HEADS))], LANES)
        qp, kp, vp, fs = _fox_pack(z_all, f_bias)
        o_a = _attention(qp, kp, vp, fs)
        wpa = jnp.pad(w_proj_a[l].reshape(N_HEADS, HEAD_DIM, d), ((0, 0), (0, LANES - HEAD_DIM), (0, 0)))
        wpa = wpa.reshape(N_HEADS * LANES, d).astype(BF16)

        lru_par = _pack_rows([lru_conv_b[l], lru_ga_b[l], lru_gx_b[l], lru_lambda[l]], LRU_WIDTH)
        conv_w = jnp.pad(lru_conv_w[l], ((0, 8 - LRU_CONV), (0, 0)))
        o_b = _lru(z_all, lru_par, conv_w, _block_diag(lru_ga_w[l]).astype(BF16),
                   _block_diag(lru_gx_w[l]).astype(BF16))

        mu = rwkv_mu[l]
        zv = lambda n: jnp.zeros((n,), F32)
        mu_pad = jnp.concatenate([
            mu[0:3 * wd], mu[3 * wd:3 * wd + lora], zv(LORA_PAD - lora),
            mu[3 * wd + lora:3 * wd + 2 * lora], zv(LORA_PAD - lora),
            mu[3 * wd + 2 * lora:], zv(RWKV_COLS_PAD - 3 * wd - 3 * LORA_PAD)])
        rw_par = _pack_rows([rwkv_w0[l], rwkv_a0[l], rwkv_k_k[l], rwkv_k_a[l], rwkv_r_k[l].reshape(wd)], wd)
        pad_rows = lambda w: jnp.pad(w, ((0, LORA_PAD - w.shape[0]), (0, 0)))
        r, ld, kf, v, an, b, bonus, g = _rwkv_prep(
            z_all, _pack_rows([mu_pad], RWKV_COLS_PAD), rw_par,
            pad_rows(rwkv_w2[l]), pad_rows(rwkv_a2[l]), rwkv_g2[l], blk)
        y = _rwkv_scan(r, ld, kf, v, an, b)

        par = _pack_rows([mods[5], ln_g[l, 1], ln_b[l, 1]], d)
        gn = _pack_rows([rwkv_gn_w[l], rwkv_gn_b[l]], wd)
        xs = _merge(xs, o_a, o_b, y, bonus, g, z_all, par, gn, blk,
                    wpa, w_proj_b[l].astype(BF16), w_proj_c[l].astype(BF16),
                    w_out[l].astype(BF16), alpha)

        xs = _ffn(xs, _pack_rows(mods[6:9], d), _pack_rows([ln_g[l, 2], ln_b[l, 2]], d),
                  ffn_up[l, 1].astype(BF16), ffn_down[l, 1].astype(BF16), alpha)

    return xs[None]
```

```python
import functools

import jax
import jax.numpy as jnp
from jax import lax
from jax.experimental import pallas as pl
from jax.experimental.pallas import tpu as pltpu

F32 = jnp.float32
BF16 = jnp.bfloat16
HI = lax.Precision.HIGHEST

LANES = 128
HEAD_DIM = 64
N_HEADS = 8
ATT_WIDTH = 512
LRU_WIDTH = 1024
LRU_BLOCK = 64
LRU_CONV = 4
LRU_C = 8.0
RWKV_WIDTH = 512
LORA_PAD = 128
RWKV_COLS_PAD = 2048
RWKV_GN_EPS = 64e-5
LN_EPS = 1e-5
NEG_BIG = -1e30
VMEM_LIMIT = 56 * 1024 * 1024

RWKV_CHUNK = 64
SCAN_GROUP_HEADS = 2
ATT_TILE = 512
ATT_SLOTS = 4
LOG2E = 1.4426950408889634

COL_QKV = 0
COL_FL = 1536
COL_LRU = 2048
COL_RWKV = 4096
COL_GATE = 6144
N_IN_PAD = 9216

LANE_ONE = HEAD_DIM
LANE_F = HEAD_DIM + 3


def _dot(a, b, prec=None):
    return jnp.dot(a, b, preferred_element_type=F32, precision=prec)


def _dot_nt(a, b, prec=None):
    return lax.dot_general(a, b, (((1,), (1,)), ((), ())), preferred_element_type=F32, precision=prec)


def _split2(x):
    hi = x.astype(BF16)
    return hi, (x - hi.astype(F32)).astype(BF16)


def _split3(x):
    hi = x.astype(BF16)
    r = x - hi.astype(F32)
    mid = r.astype(BF16)
    return hi, mid, (r - mid.astype(F32)).astype(BF16)


def _mm3(a, b):
    cross = _dot(jnp.concatenate([a[0], a[1]], axis=1), jnp.concatenate([b[1], b[0]], axis=0))
    return _dot(a[0], b[0]) + cross


def _mm3_nt(a, b):
    cross = _dot_nt(jnp.concatenate([a[0], a[1]], axis=1), jnp.concatenate([b[1], b[0]], axis=1))
    return _dot_nt(a[0], b[0]) + cross


def _dot_x3(a, b):
    return _mm3(_split2(a), _split2(b))


def _dot_exact_lhs(sel, x):
    hi, mid, lo = _split3(x)
    return _dot(sel, hi) + (_dot(sel, mid) + _dot(sel, lo))


def _dot_exact_rhs(x, sel):
    hi, mid, lo = _split3(x)
    return _dot(hi, sel) + (_dot(mid, sel) + _dot(lo, sel))


def _head_sums(x, blk):
    hi, lo = _split2(x)
    return _dot(hi, blk) + _dot(lo, blk)


def _ln(x):
    mu = jnp.mean(x, axis=-1, keepdims=True)
    xc = x - mu
    var = jnp.mean(xc * xc, axis=-1, keepdims=True)
    return xc * lax.rsqrt(var + LN_EPS)


def _softplus(x):
    return jnp.maximum(x, 0.0) + jnp.log1p(jnp.exp(-jnp.abs(x)))


def _sigmoid(x):
    return 1.0 / (1.0 + jnp.exp(-x))


def _params(sem):
    return pltpu.CompilerParams(dimension_semantics=sem, vmem_limit_bytes=VMEM_LIMIT)


def _pack_rows(rows, width):
    rows = [r.reshape(1, width).astype(F32) for r in rows]
    pad = jnp.zeros((8 - len(rows), width), F32)
    return jnp.concatenate(rows + [pad], axis=0)


def _ada_kernel(c_ref, w_ref, b_ref, o_ref):
    c = c_ref[...]
    ca = c * _sigmoid(c)
    o_ref[0] = _dot(ca, w_ref[0], HI) + b_ref[0]


def _ada_all(c, ada_w, ada_b):
    n_layers, d, n = ada_w.shape
    tn = 1152
    c8 = jnp.broadcast_to(c.astype(F32), (8, d))
    out = pl.pallas_call(
        _ada_kernel,
        grid=(n_layers, n // tn),
        in_specs=[
            pl.BlockSpec((8, d), lambda l, j: (0, 0)),
            pl.BlockSpec((1, d, tn), lambda l, j: (l, 0, j)),
            pl.BlockSpec((1, 1, tn), lambda l, j: (l, 0, j)),
        ],
        out_specs=pl.BlockSpec((1, 8, tn), lambda l, j: (l, 0, j)),
        out_shape=jax.ShapeDtypeStruct((n_layers, 8, n), F32),
        compiler_params=_params(("arbitrary", "arbitrary")),
    )(c8, ada_w, ada_b.reshape(n_layers, 1, n))
    return out[:, 0, :]


def _ffn_kernel(x_ref, mod_ref, lnp_ref, wup_ref, wd_ref, o_ref, *next_ref, alpha, tf):
    d_ff = wd_ref.shape[0]
    n_chunks = d_ff // tf
    x = x_ref[...]
    h = (_ln(x) * (1.0 + mod_ref[1:2, :]) + mod_ref[0:1, :]).astype(BF16)

    def up(c):
        u = _dot(h, wup_ref[:, c * tf:(c + 1) * tf])
        g = _dot(h, wup_ref[:, d_ff + c * tf:d_ff + (c + 1) * tf])
        return u, g

    acc = None
    nxt = up(0)
    for c in range(n_chunks):
        u, g = nxt
        if c + 1 < n_chunks:
            nxt = up(c + 1)
        act = (u * _sigmoid(u) * g).astype(BF16)
        down = _dot(act, wd_ref[c * tf:(c + 1) * tf, :])
        acc = down if acc is None else acc + down

    z = alpha * x + (0.5 * mod_ref[2:3, :]) * acc
    out = _ln(z) * lnp_ref[0:1, :] + lnp_ref[1:2, :]
    o_ref[...] = out
    if next_ref:
        next_ref[0][...] = (_ln(out) * (1.0 + lnp_ref[3:4, :]) + lnp_ref[2:3, :]).astype(BF16)


def _ffn(x, mod, lnp, w_up, w_down, alpha, emit_next):
    s, d = x.shape
    d_ff = w_down.shape[0]
    tm = min(512, s)
    rows = pl.BlockSpec((tm, d), lambda i: (i, 0))
    out_specs, out_shape = rows, jax.ShapeDtypeStruct((s, d), F32)
    if emit_next:
        out_specs, out_shape = [rows, rows], [out_shape, jax.ShapeDtypeStruct((s, d), BF16)]
    return pl.pallas_call(
        functools.partial(_ffn_kernel, alpha=alpha, tf=256),
        grid=(s // tm,),
        in_specs=[
            rows,
            pl.BlockSpec((8, d), lambda i: (0, 0)),
            pl.BlockSpec((8, d), lambda i: (0, 0)),
            pl.BlockSpec((d, 2 * d_ff), lambda i: (0, 0)),
            pl.BlockSpec((d_ff, d), lambda i: (0, 0)),
        ],
        out_specs=out_specs,
        out_shape=out_shape,
        compiler_params=_params(("parallel",)),
    )(x, mod, lnp, w_up, w_down)


def _in_proj_kernel(h_ref, w_ref, o_ref):
    o_ref[...] = _dot(h_ref[...], w_ref[...])


def _in_proj(h, w):
    s, d = h.shape
    n = w.shape[1]
    tm = min(2048, s)
    tn = 512
    return pl.pallas_call(
        _in_proj_kernel,
        grid=(s // tm, n // tn),
        in_specs=[
            pl.BlockSpec((tm, d), lambda i, j: (i, 0)),
            pl.BlockSpec((d, tn), lambda i, j: (0, j)),
        ],
        out_specs=pl.BlockSpec((tm, tn), lambda i, j: (i, j)),
        out_shape=jax.ShapeDtypeStruct((s, n), F32),
        compiler_params=_params(("parallel", "parallel")),
    )(h, w)


def _fox_pack_kernel(qkv_ref, fl_ref, bias_ref, qp_ref, kp_ref, vp_ref, fs_ref, carry):
    tm = fl_ref.shape[0]

    @pl.when(pl.program_id(0) == 0)
    def _():
        carry[...] = jnp.zeros_like(carry)

    log_f = -_softplus(-(fl_ref[...] + bias_ref[0:1, :]))
    row = lax.broadcasted_iota(jnp.int32, (tm, tm), 0)
    col = lax.broadcasted_iota(jnp.int32, (tm, tm), 1)
    tril = jnp.where(col <= row, 1.0, 0.0).astype(BF16)
    f_loc = _dot_exact_lhs(tril, log_f * LOG2E)

    start = carry[...]
    r8 = lax.broadcasted_iota(jnp.int32, (8, LANES), 0)
    c8 = lax.broadcasted_iota(jnp.int32, (8, LANES), 1)
    diag = jnp.where(r8 == c8, start, 0.0)
    fs_ref[0] = _dot_exact_rhs(diag, jnp.ones((LANES, LANES), BF16))
    carry[...] = start + f_loc[tm - 1:tm, :]

    lane = lax.broadcasted_iota(jnp.int32, (1, LANES), 1)
    head_lanes = lane < HEAD_DIM
    ones3 = jnp.logical_and(lane >= LANE_ONE, lane < LANE_ONE + 3).astype(F32)
    onesf = jnp.logical_and(lane >= LANE_F, lane < LANE_F + 3).astype(F32)
    one1 = (lane == LANE_ONE).astype(F32)
    for h in range(N_HEADS):
        f = jnp.broadcast_to(f_loc[:, h:h + 1], (tm, LANES))
        pieces = [p.astype(F32) for p in _split3(f)]
        q_extra = ones3
        k_extra = onesf
        for n, piece in enumerate(pieces):
            q_extra = q_extra + jnp.where(lane == LANE_F + n, piece, 0.0)
            k_extra = k_extra - jnp.where(lane == LANE_ONE + n, piece, 0.0)
        pair = h // 2
        heads = []
        for base in (0, ATT_WIDTH, 2 * ATT_WIDTH):
            xh = qkv_ref[:, base + pair * LANES:base + (pair + 1) * LANES]
            heads.append(pltpu.roll(xh, HEAD_DIM, axis=1) if h % 2 else xh)
        sl = slice(h * LANES, (h + 1) * LANES)
        qp_ref[:, sl] = jnp.where(head_lanes, heads[0] * (LOG2E * HEAD_DIM ** -0.5), q_extra).astype(BF16)
        kp_ref[:, sl] = jnp.where(head_lanes, heads[1], k_extra).astype(BF16)
        vp_ref[:, sl] = jnp.where(head_lanes, heads[2], one1).astype(BF16)


def _fox_pack(z_all, bias):
    s = z_all.shape[0]
    tm = min(ATT_TILE, s)
    wide = N_HEADS * LANES
    packed = jax.ShapeDtypeStruct((s, wide), BF16)
    return pl.pallas_call(
        _fox_pack_kernel,
        grid=(s // tm,),
        in_specs=[
            pl.BlockSpec((tm, 3 * ATT_WIDTH), lambda i: (i, COL_QKV // (3 * ATT_WIDTH))),
            pl.BlockSpec((tm, LANES), lambda i: (i, COL_FL // LANES)),
            pl.BlockSpec((8, LANES), lambda i: (0, 0)),
        ],
        out_specs=[pl.BlockSpec((tm, wide), lambda i: (i, 0))] * 3
        + [pl.BlockSpec((1, 8, LANES), lambda i: (i, 0, 0))],
        out_shape=[packed] * 3 + [jax.ShapeDtypeStruct((s // tm, 8, LANES), F32)],
        scratch_shapes=[pltpu.VMEM((8, LANES), F32)],
        compiler_params=_params(("arbitrary",)),
    )(z_all, z_all, bias)


def _attn_kernel(q_ref, k_ref, v_ref, fs_ref, o_ref, m_scr, acc_scr, *, tile):
    h = pl.program_id(0)
    i = pl.program_id(1)
    q = q_ref[...]
    reps = tile // LANES
    row = lax.broadcasted_iota(jnp.int32, (tile, tile), 0)
    col = lax.broadcasted_iota(jnp.int32, (tile, tile), 1)
    fs_q = fs_ref[i, pl.ds(h, 1), :]

    m_scr[...] = jnp.full(m_scr.shape, NEG_BIG, F32)
    acc_scr[...] = jnp.zeros_like(acc_scr)

    def run(tiles):
        offs = [pl.multiple_of(j * tile, tile) for j, _, _ in tiles]
        m_old = [m_scr[slot] for _, slot, _ in tiles]
        acc_old = [acc_scr[slot] for _, slot, _ in tiles]
        shift = [fs_q - fs_ref[j, pl.ds(h, 1), :] for j, _, _ in tiles]
        s = [_dot_nt(q, k_ref[pl.ds(off, tile), :]) for off in offs]
        s = [jnp.where(col <= row, x, NEG_BIG) if masked else x for x, (_, _, masked) in zip(s, tiles)]
        part = []
        for x in s:
            pm = x[:, 0:LANES]
            for n in range(1, reps):
                pm = jnp.maximum(pm, x[:, n * LANES:(n + 1) * LANES])
            part.append(pm)
        m_new = [jnp.maximum(mo, jnp.max(pm, axis=-1, keepdims=True) + sh)
                 for mo, pm, sh in zip(m_old, part, shift)]
        p = [jnp.exp2(x - pltpu.repeat(mn - sh, reps, axis=1)).astype(BF16) for x, mn, sh in zip(s, m_new, shift)]
        pv = [_dot(x, v_ref[pl.ds(off, tile), :]) for x, off in zip(p, offs)]
        for n, (_, slot, _) in enumerate(tiles):
            acc_scr[slot] = jnp.exp2(m_old[n] - m_new[n]) * acc_old[n] + pv[n]
            m_scr[slot] = m_new[n]

    def body(jj, carry):
        run([(ATT_SLOTS * jj + n, n, False) for n in range(ATT_SLOTS)])
        return carry

    groups = i // ATT_SLOTS
    lax.fori_loop(0, groups, body, 0)

    for rem in range(ATT_SLOTS):
        @pl.when(i % ATT_SLOTS == rem)
        def _(rem=rem):
            run([(ATT_SLOTS * groups + n, n, False) for n in range(rem)] + [(i, rem, True)])

    m = m_scr[0]
    for n in range(1, ATT_SLOTS):
        m = jnp.maximum(m, m_scr[n])
    acc = jnp.exp2(m_scr[0] - m) * acc_scr[0]
    for n in range(1, ATT_SLOTS):
        acc = acc + jnp.exp2(m_scr[n] - m) * acc_scr[n]
    lane = lax.broadcasted_iota(jnp.int32, (1, LANES), 1)
    o = jnp.where(lane < HEAD_DIM, acc / acc[:, LANE_ONE:LANE_ONE + 1], 0.0)
    o_ref[...] = o.astype(o_ref.dtype)


def _attention(qp, kp, vp, fs):
    s, wide = qp.shape
    tile = min(ATT_TILE, s)
    return pl.pallas_call(
        functools.partial(_attn_kernel, tile=tile),
        grid=(N_HEADS, s // tile),
        in_specs=[
            pl.BlockSpec((tile, LANES), lambda h, i: (i, h)),
            pl.BlockSpec((s, LANES), lambda h, i: (0, h)),
            pl.BlockSpec((s, LANES), lambda h, i: (0, h)),
            pl.BlockSpec(fs.shape, lambda h, i: (0, 0, 0)),
        ],
        out_specs=pl.BlockSpec((tile, LANES), lambda h, i: (i, h)),
        out_shape=jax.ShapeDtypeStruct((s, wide), BF16),
        scratch_shapes=[pltpu.VMEM((ATT_SLOTS, tile, LANES), F32), pltpu.VMEM((ATT_SLOTS, tile, LANES), F32)],
        compiler_params=_params(("parallel", "arbitrary")),
    )(qp, kp, vp, fs)


def _lru_kernel(xb_ref, yb_ref, par_ref, cw_ref, ga_ref, gx_ref, o_ref,
                xpad_scr, a_scr, u_scr, h_scr, carry_scr):
    tm, w = xb_ref.shape

    @pl.when(pl.program_id(0) == 0)
    def _():
        xpad_scr[0:8, :] = jnp.zeros((8, w), F32)
        carry_scr[...] = jnp.zeros_like(carry_scr)

    xb = xb_ref[...]
    xpad_scr[8:8 + tm, :] = xb
    xc = xb * cw_ref[LRU_CONV - 1:LRU_CONV, :] + par_ref[0:1, :]
    for d in range(1, LRU_CONV):
        xc = xc + xpad_scr[8 - d:8 - d + tm, :] * cw_ref[LRU_CONV - 1 - d:LRU_CONV - d, :]
    xpad_scr[0:8, :] = xb[tm - 8:tm, :]

    xcb = xc.astype(BF16)
    r = _sigmoid(_dot(xcb, ga_ref[...]) + par_ref[1:2, :])
    gi = _sigmoid(_dot(xcb, gx_ref[...]) + par_ref[2:3, :])
    log_a = (-LRU_C) * r * _softplus(-par_ref[3:4, :])
    a = jnp.exp(log_a)
    u = jnp.sqrt(-jnp.tanh(log_a) * (a * a + 1.0)) * (gi * xc)

    r8 = lax.broadcasted_iota(jnp.int32, (tm, w), 0) % 8
    for d in (1, 2, 4):
        keep = r8 >= d
        a_prev = jnp.where(keep, pltpu.roll(a, d, axis=0), 1.0)
        u_prev = jnp.where(keep, pltpu.roll(u, d, axis=0), 0.0)
        u = a * u_prev + u
        a = a * a_prev
    a_scr[...] = a
    u_scr[...] = u

    def slab(k, carry):
        off = pl.multiple_of(k * 8, 8)
        h8 = a_scr[pl.ds(off, 8), :] * carry + u_scr[pl.ds(off, 8), :]
        h_scr[pl.ds(off, 8), :] = h8
        return jnp.broadcast_to(h8[7:8, :], (8, w))

    carry_scr[...] = lax.fori_loop(0, tm // 8, slab, carry_scr[...], unroll=4)

    yb = yb_ref[...]
    gelu = 0.5 * yb * (1.0 + jnp.tanh(0.7978845608028654 * (yb + 0.044715 * (yb * yb * yb))))
    o_ref[...] = (h_scr[...] * gelu).astype(o_ref.dtype)


def _lru(z_lru, par, conv_w, ga, gx):
    s = z_lru.shape[0]
    w = LRU_WIDTH
    tm = min(256, s)
    return pl.pallas_call(
        _lru_kernel,
        grid=(s // tm,),
        in_specs=[
            pl.BlockSpec((tm, w), lambda i: (i, COL_LRU // w)),
            pl.BlockSpec((tm, w), lambda i: (i, COL_LRU // w + 1)),
            pl.BlockSpec((8, w), lambda i: (0, 0)),
            pl.BlockSpec((8, w), lambda i: (0, 0)),
            pl.BlockSpec((w, w), lambda i: (0, 0)),
            pl.BlockSpec((w, w), lambda i: (0, 0)),
        ],
        out_specs=pl.BlockSpec((tm, w), lambda i: (i, 0)),
        out_shape=jax.ShapeDtypeStruct((s, w), BF16),
        scratch_shapes=[
            pltpu.VMEM((tm + 8, w), F32),
            pltpu.VMEM((tm, w), F32),
            pltpu.VMEM((tm, w), F32),
            pltpu.VMEM((tm, w), F32),
            pltpu.VMEM((8, w), F32),
        ],
        compiler_params=_params(("arbitrary",)),
    )(z_lru, z_lru, par, conv_w, ga, gx)


def _rwkv_prep_kernel(z_ref, mu_ref, par_ref, w2_ref, a2_ref, g2_ref, blk_ref,
                      r_ref, ld_ref, k_ref, v_ref, an_ref, b_ref, bonus_ref, g_ref, prev_scr):
    tm = z_ref.shape[0]
    wd = RWKV_WIDTH

    @pl.when(pl.program_id(0) == 0)
    def _():
        prev_scr[...] = jnp.zeros_like(prev_scr)

    z = z_ref[...]
    row = lax.broadcasted_iota(jnp.int32, z.shape, 0)
    z_prev = jnp.where(row == 0, prev_scr[0:1, :], pltpu.roll(z, 1, axis=0))
    prev_scr[...] = jnp.broadcast_to(z[tm - 1:tm, :], prev_scr.shape)
    zs = z + (z_prev - z) * mu_ref[0:1, :]

    r = zs[:, 0:wd]
    k = zs[:, wd:2 * wd]
    v = zs[:, 2 * wd:3 * wd]
    wl = zs[:, 3 * wd:3 * wd + LORA_PAD]
    al = zs[:, 3 * wd + LORA_PAD:3 * wd + 2 * LORA_PAD]
    gl = zs[:, 3 * wd + 2 * LORA_PAD:3 * wd + 3 * LORA_PAD]
    w0, a0, k_k, k_a, r_k = (par_ref[n:n + 1, :] for n in range(5))

    w = -_softplus(-(w0 + _dot_x3(jnp.tanh(wl), w2_ref[...]))) - 0.5
    a = _sigmoid(a0 + _dot_x3(al, a2_ref[...]))
    g = _dot_x3(_sigmoid(gl), g2_ref[...])

    blk = blk_ref[...]
    kk = k * k_k
    norm = jnp.sqrt(_head_sums(kk * kk, blk))
    kk = kk / jnp.maximum(norm, 1e-12)
    kf = k * (1.0 + (a - 1.0) * k_a)

    r_ref[...] = r
    ld_ref[...] = -jnp.exp(w)
    k_ref[...] = kf
    v_ref[...] = v
    an_ref[...] = -kk
    b_ref[...] = kk * a
    bonus_ref[...] = _head_sums(r * kf * r_k, blk) * v
    g_ref[...] = g


def _rwkv_prep(zc, mu, par, w2, a2, g2, blk):
    s = zc.shape[0]
    n = RWKV_COLS_PAD
    wd = RWKV_WIDTH
    tm = min(256, s)
    full = lambda shape: pl.BlockSpec(shape, lambda i: (0, 0))
    out = jax.ShapeDtypeStruct((s, wd), F32)
    return pl.pallas_call(
        _rwkv_prep_kernel,
        grid=(s // tm,),
        in_specs=[
            pl.BlockSpec((tm, n), lambda i: (i, COL_RWKV // n)),
            full((8, n)), full((8, wd)), full((LORA_PAD, wd)), full((LORA_PAD, wd)),
            full((LORA_PAD, wd)), full((wd, wd)),
        ],
        out_specs=[pl.BlockSpec((tm, wd), lambda i: (i, 0))] * 8,
        out_shape=[out] * 8,
        scratch_shapes=[pltpu.VMEM((8, n), F32)],
        compiler_params=_params(("arbitrary",)),
    )(zc, mu, par, w2, a2, g2, blk)


def _rwkv_scan_kernel(r_ref, ld_ref, k_ref, v_ref, an_ref, b_ref, y_ref, state_scr):
    t = r_ref.shape[0]

    @pl.when(pl.program_id(0) == 0)
    def _():
        state_scr[...] = jnp.zeros_like(state_scr)

    row3 = lax.broadcasted_iota(jnp.int32, (t, 3 * t), 0)
    col3 = lax.broadcasted_iota(jnp.int32, (t, 3 * t), 1)
    tril3 = jnp.where(col3 % t <= row3, 1.0, 0.0).astype(BF16)
    ld = ld_ref[...]
    c_incl = _dot(tril3, jnp.concatenate(_split3(ld), axis=0))
    p_incl = jnp.exp(c_incl)
    p_prev = jnp.exp(c_incl - ld)
    p_inv = jnp.exp(-c_incl)
    r_s = r_ref[...] * p_incl
    a_s = an_ref[...] * p_prev
    b_s = b_ref[...] * p_inv
    k_s = k_ref[...] * p_inv
    v_all = v_ref[...]

    gh = SCAN_GROUP_HEADS
    gw = gh * HEAD_DIM
    tg = gh * t
    lane = lax.broadcasted_iota(jnp.int32, (1, gw), 1)
    head_masks = [jnp.logical_and(lane >= n * HEAD_DIM, lane < (n + 1) * HEAD_DIM) for n in range(gh)]
    rowg = lax.broadcasted_iota(jnp.int32, (tg, tg), 0)
    colg = lax.broadcasted_iota(jnp.int32, (tg, tg), 1)
    same_head = (rowg // t) == (colg // t)
    strict = jnp.logical_and(same_head, (colg % t) < (rowg % t))
    incl = jnp.logical_and(same_head, (colg % t) <= (rowg % t))

    def stack(x):
        return jnp.concatenate([jnp.where(mask, x, 0.0) for mask in head_masks], axis=0)

    groups = range(N_HEADS // gh)
    sls = [slice(g * gw, (g + 1) * gw) for g in groups]
    states = [state_scr[g] for g in groups]
    cat0 = lambda xs: jnp.concatenate(xs, axis=0)
    cat0_s = lambda p, q: (cat0([p[0], q[0]]), cat0([p[1], q[1]]))

    a2_s = [_split2(stack(a_s[:, sl])) for sl in sls]
    r2_s = [_split2(stack(r_s[:, sl])) for sl in sls]
    v2 = [stack(v_all[:, sl]) for sl in sls]
    v2_s = [_split2(x) for x in v2]
    bk_s = [_split2(cat0([stack(b_s[:, sl]), stack(k_s[:, sl])])) for sl in sls]
    ar_s = [cat0_s(a2_s[g], r2_s[g]) for g in groups]
    m = [_mm3_nt(ar_s[g], bk_s[g]) for g in groups]
    from_state = [_mm3_nt(ar_s[g], _split2(states[g])) for g in groups]
    lt_k = [cat0([jnp.where(strict, m[g][0:tg, tg:2 * tg], 0.0),
                  jnp.where(incl, m[g][tg:2 * tg, tg:2 * tg], 0.0)]) for g in groups]
    from_v = [_mm3(_split2(lt_k[g]), v2_s[g]) for g in groups]

    sol = [from_state[g][0:tg] + from_v[g][0:tg] for g in groups]
    power = [jnp.where(strict, m[g][0:tg, 0:tg], 0.0) for g in groups]
    steps = max(1, (t - 1).bit_length())
    for n in range(steps):
        last = n == steps - 1
        for g in groups:
            rhs_n = sol[g] if last else jnp.concatenate([sol[g], power[g]], axis=1)
            prod = _mm3(_split2(power[g]), _split2(rhs_n))
            sol[g] = sol[g] + prod[:, 0:gw]
            if not last:
                power[g] = prod[:, gw:gw + tg]

    for g in groups:
        t_rb = jnp.where(incl, m[g][tg:2 * tg, 0:tg], 0.0)
        y2 = from_state[g][tg:2 * tg] + from_v[g][tg:2 * tg] + _mm3(_split2(t_rb), _split2(sol[g]))
        y = y2[0:t, :]
        for n in range(1, gh):
            y = y + y2[n * t:(n + 1) * t, :]
        y_ref[:, sls[g]] = y
    upd = [_mm3(_split2(cat0([sol[g], v2[g]]).T), bk_s[g]) for g in groups]
    for g in groups:
        state_scr[g] = (states[g] + upd[g]) * p_incl[t - 1:t, sls[g]]


def _rwkv_scan(r, ld, k, v, an, b):
    s, wd = r.shape
    t = min(RWKV_CHUNK, s)
    spec = pl.BlockSpec((t, wd), lambda i: (i, 0))
    return pl.pallas_call(
        _rwkv_scan_kernel,
        grid=(s // t,),
        in_specs=[spec] * 6,
        out_specs=spec,
        out_shape=jax.ShapeDtypeStruct((s, wd), F32),
        scratch_shapes=[pltpu.VMEM((N_HEADS // SCAN_GROUP_HEADS,) + (SCAN_GROUP_HEADS * HEAD_DIM,) * 2, F32)],
        compiler_params=_params(("arbitrary",)),
    )(r, ld, k, v, an, b)


def _merge_kernel(x_ref, oa_ref, ob_ref, y_ref, bonus_ref, g_ref, ga_ref, gb_ref, gc_ref,
                  par_ref, gn_ref, blk_ref, wpa_ref, wpb_ref, wpc_ref, wo_ref, o_ref, *, alpha):
    blk = blk_ref[...]
    y = y_ref[...]
    mean = _dot_exact_rhs(y, blk) * (1.0 / HEAD_DIM)
    yc = y - mean
    var = _head_sums(yc * yc, blk) * (1.0 / HEAD_DIM)
    yn = yc * lax.rsqrt(var + RWKV_GN_EPS) * gn_ref[0:1, :] + gn_ref[1:2, :]
    oc = ((yn + bonus_ref[...]) * g_ref[...]).astype(BF16)

    merged = (_sigmoid(ga_ref[...]) * _dot(oa_ref[...], wpa_ref[...])
              + _sigmoid(gb_ref[...]) * _dot(ob_ref[...], wpb_ref[...])
              + _sigmoid(gc_ref[...]) * _dot(oc, wpc_ref[...]))
    out = _dot(merged.astype(BF16), wo_ref[...])
    z = alpha * x_ref[...] + par_ref[0:1, :] * out
    o_ref[...] = _ln(z) * par_ref[1:2, :] + par_ref[2:3, :]


def _merge(x, oa, ob, y, bonus, g, z_all, par, gn, blk, wpa, wpb, wpc, wo, alpha):
    s, d = x.shape
    wd = RWKV_WIDTH
    tm = min(256, s)
    rows = lambda width, col=0: pl.BlockSpec((tm, width), lambda i, col=col: (i, col))
    full = lambda shape: pl.BlockSpec(shape, lambda i: (0, 0))
    gate0 = COL_GATE // d
    return pl.pallas_call(
        functools.partial(_merge_kernel, alpha=alpha),
        grid=(s // tm,),
        in_specs=[
            rows(d), rows(oa.shape[1]), rows(LRU_WIDTH), rows(wd), rows(wd), rows(wd),
            rows(d, gate0), rows(d, gate0 + 1), rows(d, gate0 + 2),
            full((8, d)), full((8, wd)), full((wd, wd)),
            full(wpa.shape), full((LRU_WIDTH, d)), full((wd, d)), full((d, d)),
        ],
        out_specs=rows(d),
        out_shape=jax.ShapeDtypeStruct((s, d), F32),
        compiler_params=_params(("parallel",)),
    )(x, oa, ob, y, bonus, g, z_all, z_all, z_all, par, gn, blk, wpa, wpb, wpc, wo)


def _block_diag(w):
    nb, n, _ = w.shape
    eye = jnp.eye(nb, dtype=w.dtype)
    return (eye[:, None, :, None] * w[:, :, None, :]).reshape(nb * n, nb * n)


def kernel(x, c, ada_w, ada_b, ln_g, ln_b, ffn_up, ffn_down, w_in, fox_f_bias, lru_conv_w, lru_conv_b, lru_ga_w, lru_ga_b, lru_gx_w, lru_gx_b, lru_lambda, rwkv_mu, rwkv_w0, rwkv_w2, rwkv_a0, rwkv_a2, rwkv_g2, rwkv_k_k, rwkv_k_a, rwkv_r_k, rwkv_gn_w, rwkv_gn_b, w_proj_a, w_proj_b, w_proj_c, w_out):
    batch, s, d = x.shape
    assert batch == 1 and d == 1024
    depth = ada_w.shape[0]
    alpha = float((2 * depth) ** 0.25)
    wd = RWKV_WIDTH

    ada = _ada_all(c, ada_w, ada_b)
    blk = _block_diag(jnp.ones((N_HEADS, HEAD_DIM, HEAD_DIM), BF16))

    o_att = 3 * ATT_WIDTH
    o_lru = o_att + N_HEADS
    o_rwkv = o_lru + 2 * LRU_WIDTH
    o_gate = o_rwkv + 3 * wd + 64 + 64 + 128
    lora = 64

    xs = x[0]
    for l in range(depth):
        mods = [ada[l, n * d:(n + 1) * d] for n in range(9)]
        w_l = w_in[l]

        xs, h_mix = _ffn(xs, _pack_rows(mods[0:3], d),
                         _pack_rows([ln_g[l, 0], ln_b[l, 0], mods[3], mods[4]], d),
                         ffn_up[l, 0].astype(BF16), ffn_down[l, 0].astype(BF16), alpha, True)

        w_c = w_l[:, o_rwkv:o_gate]
        zpad = lambda n: jnp.zeros((d, n), F32)
        w_all = jnp.concatenate([
            w_l[:, 0:o_att],
            w_l[:, o_att:o_lru], zpad(COL_LRU - COL_FL - N_HEADS),
            w_l[:, o_lru:o_rwkv],
            w_c[:, 0:3 * wd],
            w_c[:, 3 * wd:3 * wd + lora], zpad(LORA_PAD - lora),
            w_c[:, 3 * wd + lora:3 * wd + 2 * lora], zpad(LORA_PAD - lora),
            w_c[:, 3 * wd + 2 * lora:], zpad(RWKV_COLS_PAD - 3 * wd - 3 * LORA_PAD),
            w_l[:, o_gate:]], axis=1).astype(BF16)
        assert w_all.shape[1] == N_IN_PAD
        z_all = _in_proj(h_mix, w_all)

        f_bias = _pack_rows([jnp.pad(fox_f_bias[l], (0, LANES - N_HEADS))], LANES)
        qp, kp, vp, fs = _fox_pack(z_all, f_bias)
        o_a = _attention(qp, kp, vp, fs)
        wpa = jnp.pad(w_proj_a[l].reshape(N_HEADS, HEAD_DIM, d), ((0, 0), (0, LANES - HEAD_DIM), (0, 0)))
        wpa = wpa.reshape(N_HEADS * LANES, d).astype(BF16)

        lru_par = _pack_rows([lru_conv_b[l], lru_ga_b[l], lru_gx_b[l], lru_lambda[l]], LRU_WIDTH)
        conv_w = jnp.pad(lru_conv_w[l], ((0, 8 - LRU_CONV), (0, 0)))
        o_b = _lru(z_all, lru_par, conv_w, _block_diag(lru_ga_w[l]).astype(BF16),
                   _block_diag(lru_gx_w[l]).astype(BF16))

        mu = rwkv_mu[l]
        zv = lambda n: jnp.zeros((n,), F32)
        mu_pad = jnp.concatenate([
            mu[0:3 * wd], mu[3 * wd:3 * wd + lora], zv(LORA_PAD - lora),
            mu[3 * wd + lora:3 * wd + 2 * lora], zv(LORA_PAD - lora),
            mu[3 * wd + 2 * lora:], zv(RWKV_COLS_PAD - 3 * wd - 3 * LORA_PAD)])
        rw_par = _pack_rows([rwkv_w0[l], rwkv_a0[l], rwkv_k_k[l], rwkv_k_a[l], rwkv_r_k[l].reshape(wd)], wd)
        pad_rows = lambda w: jnp.pad(w, ((0, LORA_PAD - w.shape[0]), (0, 0)))
        r, ld, kf, v, an, b, bonus, g = _rwkv_prep(
            z_all, _pack_rows([mu_pad], RWKV_COLS_PAD), rw_par,
            pad_rows(rwkv_w2[l]), pad_rows(rwkv_a2[l]), rwkv_g2[l], blk)
        y = _rwkv_scan(r, ld, kf, v, an, b)

        par = _pack_rows([mods[5], ln_g[l, 1], ln_b[l, 1]], d)
        gn = _pack_rows([rwkv_gn_w[l], rwkv_gn_b[l]], wd)
        xs = _merge(xs, o_a, o_b, y, bonus, g, z_all, par, gn, blk,
                    wpa, w_proj_b[l].astype(BF16), w_proj_c[l].astype(BF16),
                    w_out[l].astype(BF16), alpha)

        xs = _ffn(xs, _pack_rows(mods[6:9], d), _pack_rows([ln_g[l, 2], ln_b[l, 2]], d),
                  ffn_up[l, 1].astype(BF16), ffn_down[l, 1].astype(BF16), alpha, False)

    return xs[None]
```

```python
import functools

import jax
import jax.numpy as jnp
from jax import lax
from jax.experimental import pallas as pl
from jax.experimental.pallas import tpu as pltpu

F32 = jnp.float32
BF16 = jnp.bfloat16
HI = lax.Precision.HIGHEST

LANES = 128
HEAD_DIM = 64
N_HEADS = 8
ATT_WIDTH = 512
LRU_WIDTH = 1024
LRU_BLOCK = 64
LRU_CONV = 4
LRU_C = 8.0
RWKV_WIDTH = 512
LORA_PAD = 128
RWKV_COLS_PAD = 2048
RWKV_GN_EPS = 64e-5
LN_EPS = 1e-5
NEG_BIG = -1e30
VMEM_LIMIT = 56 * 1024 * 1024

RWKV_CHUNK = 64
SCAN_BASE_BLOCK = 8
SCAN_GROUP_HEADS = 2
ATT_TILE = 512
ATT_SLOTS = 8
LOG2E = 1.4426950408889634

COL_QKV = 0
COL_FL = 1536
COL_LRU = 2048
COL_RWKV = 4096
COL_GATE = 6144
N_IN_PAD = 9216

LANE_ONE = HEAD_DIM
LANE_F = HEAD_DIM + 3


def _dot(a, b, prec=None):
    return jnp.dot(a, b, preferred_element_type=F32, precision=prec)


def _dot_nt(a, b, prec=None):
    return lax.dot_general(a, b, (((1,), (1,)), ((), ())), preferred_element_type=F32, precision=prec)


def _split2(x):
    hi = x.astype(BF16)
    return hi, (x - hi.astype(F32)).astype(BF16)


def _split3(x):
    hi = x.astype(BF16)
    r = x - hi.astype(F32)
    mid = r.astype(BF16)
    return hi, mid, (r - mid.astype(F32)).astype(BF16)


def _mm3(a, b):
    cross = _dot(jnp.concatenate([a[0], a[1]], axis=1), jnp.concatenate([b[1], b[0]], axis=0))
    return _dot(a[0], b[0]) + cross


def _mm3_nt(a, b):
    cross = _dot_nt(jnp.concatenate([a[0], a[1]], axis=1), jnp.concatenate([b[1], b[0]], axis=1))
    return _dot_nt(a[0], b[0]) + cross


def _dot_x3(a, b):
    return _mm3(_split2(a), _split2(b))


def _dot_exact_lhs(sel, x):
    hi, mid, lo = _split3(x)
    return _dot(sel, hi) + (_dot(sel, mid) + _dot(sel, lo))


def _dot_exact_rhs(x, sel):
    hi, mid, lo = _split3(x)
    return _dot(hi, sel) + (_dot(mid, sel) + _dot(lo, sel))


def _head_sums(x, blk):
    hi, lo = _split2(x)
    return _dot(hi, blk) + _dot(lo, blk)


def _ln(x):
    mu = jnp.mean(x, axis=-1, keepdims=True)
    xc = x - mu
    var = jnp.mean(xc * xc, axis=-1, keepdims=True)
    return xc * lax.rsqrt(var + LN_EPS)


def _softplus(x):
    return jnp.maximum(x, 0.0) + jnp.log1p(jnp.exp(-jnp.abs(x)))


def _sigmoid(x):
    return 1.0 / (1.0 + jnp.exp(-x))


def _params(sem):
    return pltpu.CompilerParams(dimension_semantics=sem, vmem_limit_bytes=VMEM_LIMIT)


def _pack_rows(rows, width):
    rows = [r.reshape(1, width).astype(F32) for r in rows]
    pad = jnp.zeros((8 - len(rows), width), F32)
    return jnp.concatenate(rows + [pad], axis=0)


def _ada_kernel(c_ref, w_ref, b_ref, o_ref):
    c = c_ref[...]
    ca = c * _sigmoid(c)
    o_ref[0] = _dot(ca, w_ref[0], HI) + b_ref[0]


def _ada_all(c, ada_w, ada_b):
    n_layers, d, n = ada_w.shape
    tn = 1152
    c8 = jnp.broadcast_to(c.astype(F32), (8, d))
    out = pl.pallas_call(
        _ada_kernel,
        grid=(n_layers, n // tn),
        in_specs=[
            pl.BlockSpec((8, d), lambda l, j: (0, 0)),
            pl.BlockSpec((1, d, tn), lambda l, j: (l, 0, j)),
            pl.BlockSpec((1, 1, tn), lambda l, j: (l, 0, j)),
        ],
        out_specs=pl.BlockSpec((1, 8, tn), lambda l, j: (l, 0, j)),
        out_shape=jax.ShapeDtypeStruct((n_layers, 8, n), F32),
        compiler_params=_params(("arbitrary", "arbitrary")),
    )(c8, ada_w, ada_b.reshape(n_layers, 1, n))
    return out[:, 0, :]


def _ffn_kernel(x_ref, mod_ref, lnp_ref, wup_ref, wd_ref, o_ref, *next_ref, alpha, tf):
    d_ff = wd_ref.shape[0]
    n_chunks = d_ff // tf
    x = x_ref[...]
    h = (_ln(x) * (1.0 + mod_ref[1:2, :]) + mod_ref[0:1, :]).astype(BF16)

    def up(c):
        u = _dot(h, wup_ref[:, c * tf:(c + 1) * tf])
        g = _dot(h, wup_ref[:, d_ff + c * tf:d_ff + (c + 1) * tf])
        return u, g

    acc = None
    nxt = up(0)
    for c in range(n_chunks):
        u, g = nxt
        if c + 1 < n_chunks:
            nxt = up(c + 1)
        act = (u * _sigmoid(u) * g).astype(BF16)
        down = _dot(act, wd_ref[c * tf:(c + 1) * tf, :])
        acc = down if acc is None else acc + down

    z = alpha * x + (0.5 * mod_ref[2:3, :]) * acc
    out = _ln(z) * lnp_ref[0:1, :] + lnp_ref[1:2, :]
    o_ref[...] = out
    if next_ref:
        next_ref[0][...] = (_ln(out) * (1.0 + lnp_ref[3:4, :]) + lnp_ref[2:3, :]).astype(BF16)


def _ffn(x, mod, lnp, w_up, w_down, alpha, emit_next):
    s, d = x.shape
    d_ff = w_down.shape[0]
    tm = min(512, s)
    rows = pl.BlockSpec((tm, d), lambda i: (i, 0))
    out_specs, out_shape = rows, jax.ShapeDtypeStruct((s, d), F32)
    if emit_next:
        out_specs, out_shape = [rows, rows], [out_shape, jax.ShapeDtypeStruct((s, d), BF16)]
    return pl.pallas_call(
        functools.partial(_ffn_kernel, alpha=alpha, tf=256),
        grid=(s // tm,),
        in_specs=[
            rows,
            pl.BlockSpec((8, d), lambda i: (0, 0)),
            pl.BlockSpec((8, d), lambda i: (0, 0)),
            pl.BlockSpec((d, 2 * d_ff), lambda i: (0, 0)),
            pl.BlockSpec((d_ff, d), lambda i: (0, 0)),
        ],
        out_specs=out_specs,
        out_shape=out_shape,
        compiler_params=_params(("parallel",)),
    )(x, mod, lnp, w_up, w_down)


def _in_proj_kernel(h_ref, w_ref, o_ref):
    o_ref[...] = _dot(h_ref[...], w_ref[...])


def _in_proj(h, w):
    s, d = h.shape
    n = w.shape[1]
    tm = min(2048, s)
    tn = 512
    return pl.pallas_call(
        _in_proj_kernel,
        grid=(s // tm, n // tn),
        in_specs=[
            pl.BlockSpec((tm, d), lambda i, j: (i, 0)),
            pl.BlockSpec((d, tn), lambda i, j: (0, j)),
        ],
        out_specs=pl.BlockSpec((tm, tn), lambda i, j: (i, j)),
        out_shape=jax.ShapeDtypeStruct((s, n), F32),
        compiler_params=_params(("parallel", "parallel")),
    )(h, w)


def _fox_pack_kernel(qkv_ref, fl_ref, bias_ref, qp_ref, kp_ref, vp_ref, fs_ref, carry):
    tm = fl_ref.shape[0]

    @pl.when(pl.program_id(0) == 0)
    def _():
        carry[...] = jnp.zeros_like(carry)

    log_f = -_softplus(-(fl_ref[...] + bias_ref[0:1, :]))
    row = lax.broadcasted_iota(jnp.int32, (tm, tm), 0)
    col = lax.broadcasted_iota(jnp.int32, (tm, tm), 1)
    tril = jnp.where(col <= row, 1.0, 0.0).astype(BF16)
    f_loc = _dot_exact_lhs(tril, log_f * LOG2E)

    start = carry[...]
    r8 = lax.broadcasted_iota(jnp.int32, (8, LANES), 0)
    c8 = lax.broadcasted_iota(jnp.int32, (8, LANES), 1)
    diag = jnp.where(r8 == c8, start, 0.0)
    fs_ref[0] = _dot_exact_rhs(diag, jnp.ones((LANES, LANES), BF16))
    carry[...] = start + f_loc[tm - 1:tm, :]

    lane = lax.broadcasted_iota(jnp.int32, (1, LANES), 1)
    head_lanes = lane < HEAD_DIM
    ones3 = jnp.logical_and(lane >= LANE_ONE, lane < LANE_ONE + 3).astype(F32)
    onesf = jnp.logical_and(lane >= LANE_F, lane < LANE_F + 3).astype(F32)
    one1 = (lane == LANE_ONE).astype(F32)
    for h in range(N_HEADS):
        f = jnp.broadcast_to(f_loc[:, h:h + 1], (tm, LANES))
        pieces = [p.astype(F32) for p in _split3(f)]
        q_extra = ones3
        k_extra = onesf
        for n, piece in enumerate(pieces):
            q_extra = q_extra + jnp.where(lane == LANE_F + n, piece, 0.0)
            k_extra = k_extra - jnp.where(lane == LANE_ONE + n, piece, 0.0)
        pair = h // 2
        heads = []
        for base in (0, ATT_WIDTH, 2 * ATT_WIDTH):
            xh = qkv_ref[:, base + pair * LANES:base + (pair + 1) * LANES]
            heads.append(pltpu.roll(xh, HEAD_DIM, axis=1) if h % 2 else xh)
        sl = slice(h * LANES, (h + 1) * LANES)
        qp_ref[:, sl] = jnp.where(head_lanes, heads[0] * (LOG2E * HEAD_DIM ** -0.5), q_extra).astype(BF16)
        kp_ref[:, sl] = jnp.where(head_lanes, heads[1], k_extra).astype(BF16)
        vp_ref[:, sl] = jnp.where(head_lanes, heads[2], one1).astype(BF16)


def _fox_pack(z_all, bias):
    s = z_all.shape[0]
    tm = min(ATT_TILE, s)
    wide = N_HEADS * LANES
    packed = jax.ShapeDtypeStruct((s, wide), BF16)
    return pl.pallas_call(
        _fox_pack_kernel,
        grid=(s // tm,),
        in_specs=[
            pl.BlockSpec((tm, 3 * ATT_WIDTH), lambda i: (i, COL_QKV // (3 * ATT_WIDTH))),
            pl.BlockSpec((tm, LANES), lambda i: (i, COL_FL // LANES)),
            pl.BlockSpec((8, LANES), lambda i: (0, 0)),
        ],
        out_specs=[pl.BlockSpec((tm, wide), lambda i: (i, 0))] * 3
        + [pl.BlockSpec((1, 8, LANES), lambda i: (i, 0, 0))],
        out_shape=[packed] * 3 + [jax.ShapeDtypeStruct((s // tm, 8, LANES), F32)],
        scratch_shapes=[pltpu.VMEM((8, LANES), F32)],
        compiler_params=_params(("arbitrary",)),
    )(z_all, z_all, bias)


def _attn_kernel(q_ref, k_ref, v_ref, fs_ref, o_ref, m_scr, acc_scr, *, tile):
    h = pl.program_id(0)
    i = pl.program_id(1)
    q = q_ref[...]
    reps = tile // LANES
    row = lax.broadcasted_iota(jnp.int32, (tile, tile), 0)
    col = lax.broadcasted_iota(jnp.int32, (tile, tile), 1)
    fs_q = fs_ref[i, pl.ds(h, 1), :]

    m_scr[...] = jnp.full(m_scr.shape, NEG_BIG, F32)
    acc_scr[...] = jnp.zeros_like(acc_scr)

    def run(tiles):
        offs = [pl.multiple_of(j * tile, tile) for j, _, _ in tiles]
        m_old = [m_scr[slot] for _, slot, _ in tiles]
        acc_old = [acc_scr[slot] for _, slot, _ in tiles]
        shift = [fs_q - fs_ref[j, pl.ds(h, 1), :] for j, _, _ in tiles]
        s = [_dot_nt(q, k_ref[pl.ds(off, tile), :]) for off in offs]
        s = [jnp.where(col <= row, x, NEG_BIG) if masked else x for x, (_, _, masked) in zip(s, tiles)]
        part = []
        for x in s:
            pm = x[:, 0:LANES]
            for n in range(1, reps):
                pm = jnp.maximum(pm, x[:, n * LANES:(n + 1) * LANES])
            part.append(pm)
        m_new = [jnp.maximum(mo, jnp.max(pm, axis=-1, keepdims=True) + sh)
                 for mo, pm, sh in zip(m_old, part, shift)]
        p = [jnp.exp2(x - pltpu.repeat(mn - sh, reps, axis=1)).astype(BF16) for x, mn, sh in zip(s, m_new, shift)]
        pv = [_dot(x, v_ref[pl.ds(off, tile), :]) for x, off in zip(p, offs)]
        for n, (_, slot, _) in enumerate(tiles):
            acc_scr[slot] = jnp.exp2(m_old[n] - m_new[n]) * acc_old[n] + pv[n]
            m_scr[slot] = m_new[n]

    def body(jj, carry):
        run([(ATT_SLOTS * jj + n, n, False) for n in range(ATT_SLOTS)])
        return carry

    groups = i // ATT_SLOTS
    lax.fori_loop(0, groups, body, 0)

    for rem in range(ATT_SLOTS):
        @pl.when(i % ATT_SLOTS == rem)
        def _(rem=rem):
            run([(ATT_SLOTS * groups + n, n, False) for n in range(rem)] + [(i, rem, True)])

    m = m_scr[0]
    for n in range(1, ATT_SLOTS):
        m = jnp.maximum(m, m_scr[n])
    acc = jnp.exp2(m_scr[0] - m) * acc_scr[0]
    for n in range(1, ATT_SLOTS):
        acc = acc + jnp.exp2(m_scr[n] - m) * acc_scr[n]
    lane = lax.broadcasted_iota(jnp.int32, (1, LANES), 1)
    o = jnp.where(lane < HEAD_DIM, acc / acc[:, LANE_ONE:LANE_ONE + 1], 0.0)
    o_ref[...] = o.astype(o_ref.dtype)


def _attention(qp, kp, vp, fs):
    s, wide = qp.shape
    tile = min(ATT_TILE, s)
    return pl.pallas_call(
        functools.partial(_attn_kernel, tile=tile),
        grid=(N_HEADS, s // tile),
        in_specs=[
            pl.BlockSpec((tile, LANES), lambda h, i: (i, h)),
            pl.BlockSpec((s, LANES), lambda h, i: (0, h)),
            pl.BlockSpec((s, LANES), lambda h, i: (0, h)),
            pl.BlockSpec(fs.shape, lambda h, i: (0, 0, 0)),
        ],
        out_specs=pl.BlockSpec((tile, LANES), lambda h, i: (i, h)),
        out_shape=jax.ShapeDtypeStruct((s, wide), BF16),
        scratch_shapes=[pltpu.VMEM((ATT_SLOTS, tile, LANES), F32), pltpu.VMEM((ATT_SLOTS, tile, LANES), F32)],
        compiler_params=_params(("parallel", "arbitrary")),
    )(qp, kp, vp, fs)


def _lru_kernel(xb_ref, yb_ref, par_ref, cw_ref, ga_ref, gx_ref, o_ref,
                xpad_scr, a_scr, u_scr, h_scr, carry_scr):
    tm, w = xb_ref.shape

    @pl.when(pl.program_id(0) == 0)
    def _():
        xpad_scr[0:8, :] = jnp.zeros((8, w), F32)
        carry_scr[...] = jnp.zeros_like(carry_scr)

    xb = xb_ref[...]
    xpad_scr[8:8 + tm, :] = xb
    xc = xb * cw_ref[LRU_CONV - 1:LRU_CONV, :] + par_ref[0:1, :]
    for d in range(1, LRU_CONV):
        xc = xc + xpad_scr[8 - d:8 - d + tm, :] * cw_ref[LRU_CONV - 1 - d:LRU_CONV - d, :]
    xpad_scr[0:8, :] = xb[tm - 8:tm, :]

    xcb = xc.astype(BF16)
    r = _sigmoid(_dot(xcb, ga_ref[...]) + par_ref[1:2, :])
    gi = _sigmoid(_dot(xcb, gx_ref[...]) + par_ref[2:3, :])
    log_a = (-LRU_C) * r * _softplus(-par_ref[3:4, :])
    a = jnp.exp(log_a)
    u = jnp.sqrt(-jnp.tanh(log_a) * (a * a + 1.0)) * (gi * xc)

    r8 = lax.broadcasted_iota(jnp.int32, (tm, w), 0) % 8
    for d in (1, 2, 4):
        keep = r8 >= d
        a_prev = jnp.where(keep, pltpu.roll(a, d, axis=0), 1.0)
        u_prev = jnp.where(keep, pltpu.roll(u, d, axis=0), 0.0)
        u = a * u_prev + u
        a = a * a_prev
    a_scr[...] = a
    u_scr[...] = u

    def slab(k, carry):
        off = pl.multiple_of(k * 8, 8)
        h8 = a_scr[pl.ds(off, 8), :] * carry + u_scr[pl.ds(off, 8), :]
        h_scr[pl.ds(off, 8), :] = h8
        return jnp.broadcast_to(h8[7:8, :], (8, w))

    carry_scr[...] = lax.fori_loop(0, tm // 8, slab, carry_scr[...], unroll=4)

    yb = yb_ref[...]
    gelu = 0.5 * yb * (1.0 + jnp.tanh(0.7978845608028654 * (yb + 0.044715 * (yb * yb * yb))))
    o_ref[...] = (h_scr[...] * gelu).astype(o_ref.dtype)


def _lru(z_lru, par, conv_w, ga, gx):
    s = z_lru.shape[0]
    w = LRU_WIDTH
    tm = min(256, s)
    return pl.pallas_call(
        _lru_kernel,
        grid=(s // tm,),
        in_specs=[
            pl.BlockSpec((tm, w), lambda i: (i, COL_LRU // w)),
            pl.BlockSpec((tm, w), lambda i: (i, COL_LRU // w + 1)),
            pl.BlockSpec((8, w), lambda i: (0, 0)),
            pl.BlockSpec((8, w), lambda i: (0, 0)),
            pl.BlockSpec((w, w), lambda i: (0, 0)),
            pl.BlockSpec((w, w), lambda i: (0, 0)),
        ],
        out_specs=pl.BlockSpec((tm, w), lambda i: (i, 0)),
        out_shape=jax.ShapeDtypeStruct((s, w), BF16),
        scratch_shapes=[
            pltpu.VMEM((tm + 8, w), F32),
            pltpu.VMEM((tm, w), F32),
            pltpu.VMEM((tm, w), F32),
            pltpu.VMEM((tm, w), F32),
            pltpu.VMEM((8, w), F32),
        ],
        compiler_params=_params(("arbitrary",)),
    )(z_lru, z_lru, par, conv_w, ga, gx)


def _rwkv_prep_kernel(z_ref, mu_ref, par_ref, w2_ref, a2_ref, g2_ref, blk_ref,
                      r_ref, ld_ref, k_ref, v_ref, an_ref, b_ref, bonus_ref, g_ref, prev_scr):
    tm = z_ref.shape[0]
    wd = RWKV_WIDTH

    @pl.when(pl.program_id(0) == 0)
    def _():
        prev_scr[...] = jnp.zeros_like(prev_scr)

    z = z_ref[...]
    row = lax.broadcasted_iota(jnp.int32, z.shape, 0)
    z_prev = jnp.where(row == 0, prev_scr[0:1, :], pltpu.roll(z, 1, axis=0))
    prev_scr[...] = jnp.broadcast_to(z[tm - 1:tm, :], prev_scr.shape)
    zs = z + (z_prev - z) * mu_ref[0:1, :]

    r = zs[:, 0:wd]
    k = zs[:, wd:2 * wd]
    v = zs[:, 2 * wd:3 * wd]
    wl = zs[:, 3 * wd:3 * wd + LORA_PAD]
    al = zs[:, 3 * wd + LORA_PAD:3 * wd + 2 * LORA_PAD]
    gl = zs[:, 3 * wd + 2 * LORA_PAD:3 * wd + 3 * LORA_PAD]
    w0, a0, k_k, k_a, r_k = (par_ref[n:n + 1, :] for n in range(5))

    w = -_softplus(-(w0 + _dot_x3(jnp.tanh(wl), w2_ref[...]))) - 0.5
    a = _sigmoid(a0 + _dot_x3(al, a2_ref[...]))
    g = _dot_x3(_sigmoid(gl), g2_ref[...])

    blk = blk_ref[...]
    kk = k * k_k
    norm = jnp.sqrt(_head_sums(kk * kk, blk))
    kk = kk / jnp.maximum(norm, 1e-12)
    kf = k * (1.0 + (a - 1.0) * k_a)

    r_ref[...] = r
    ld_ref[...] = -jnp.exp(w)
    k_ref[...] = kf
    v_ref[...] = v
    an_ref[...] = -kk
    b_ref[...] = kk * a
    bonus_ref[...] = _head_sums(r * kf * r_k, blk) * v
    g_ref[...] = g


def _rwkv_prep(zc, mu, par, w2, a2, g2, blk):
    s = zc.shape[0]
    n = RWKV_COLS_PAD
    wd = RWKV_WIDTH
    tm = min(256, s)
    full = lambda shape: pl.BlockSpec(shape, lambda i: (0, 0))
    out = jax.ShapeDtypeStruct((s, wd), F32)
    return pl.pallas_call(
        _rwkv_prep_kernel,
        grid=(s // tm,),
        in_specs=[
            pl.BlockSpec((tm, n), lambda i: (i, COL_RWKV // n)),
            full((8, n)), full((8, wd)), full((LORA_PAD, wd)), full((LORA_PAD, wd)),
            full((LORA_PAD, wd)), full((wd, wd)),
        ],
        out_specs=[pl.BlockSpec((tm, wd), lambda i: (i, 0))] * 8,
        out_shape=[out] * 8,
        scratch_shapes=[pltpu.VMEM((8, n), F32)],
        compiler_params=_params(("arbitrary",)),
    )(zc, mu, par, w2, a2, g2, blk)


def _rwkv_scan_kernel(r_ref, ld_ref, k_ref, v_ref, an_ref, b_ref, y_ref, state_scr):
    t = r_ref.shape[0]

    @pl.when(pl.program_id(0) == 0)
    def _():
        state_scr[...] = jnp.zeros_like(state_scr)

    row3 = lax.broadcasted_iota(jnp.int32, (t, 3 * t), 0)
    col3 = lax.broadcasted_iota(jnp.int32, (t, 3 * t), 1)
    tril3 = jnp.where(col3 % t <= row3, 1.0, 0.0).astype(BF16)
    ld = ld_ref[...]
    c_incl = _dot(tril3, jnp.concatenate(_split3(ld), axis=0))
    p_incl = jnp.exp(c_incl)
    p_prev = jnp.exp(c_incl - ld)
    p_inv = jnp.exp(-c_incl)
    r_s = r_ref[...] * p_incl
    a_s = an_ref[...] * p_prev
    b_s = b_ref[...] * p_inv
    k_s = k_ref[...] * p_inv
    v_all = v_ref[...]

    gh = SCAN_GROUP_HEADS
    gw = gh * HEAD_DIM
    tg = gh * t
    lane = lax.broadcasted_iota(jnp.int32, (1, gw), 1)
    head_masks = [jnp.logical_and(lane >= n * HEAD_DIM, lane < (n + 1) * HEAD_DIM) for n in range(gh)]
    rowg = lax.broadcasted_iota(jnp.int32, (tg, tg), 0)
    colg = lax.broadcasted_iota(jnp.int32, (tg, tg), 1)
    same_head = (rowg // t) == (colg // t)
    strict = jnp.logical_and(same_head, (colg % t) < (rowg % t))
    incl = jnp.logical_and(same_head, (colg % t) <= (rowg % t))

    def stack(x):
        return jnp.concatenate([jnp.where(mask, x, 0.0) for mask in head_masks], axis=0)

    groups = range(N_HEADS // gh)
    sls = [slice(g * gw, (g + 1) * gw) for g in groups]
    states = [state_scr[g] for g in groups]
    cat0 = lambda xs: jnp.concatenate(xs, axis=0)

    ar = [cat0([stack(a_s[:, sl]), stack(r_s[:, sl])]).astype(BF16) for sl in sls]
    v2 = [stack(v_all[:, sl]) for sl in sls]
    v2_b = [x.astype(BF16) for x in v2]
    bk = [cat0([stack(b_s[:, sl]), stack(k_s[:, sl])]).astype(BF16) for sl in sls]
    m = [_dot_nt(ar[g], bk[g]) for g in groups]
    from_state = [_dot_nt(ar[g], states[g].astype(BF16)) for g in groups]
    lt_k = [cat0([jnp.where(strict, m[g][0:tg, tg:2 * tg], 0.0),
                  jnp.where(incl, m[g][tg:2 * tg, tg:2 * tg], 0.0)]).astype(BF16) for g in groups]
    from_v = [_dot(lt_k[g], v2_b[g]) for g in groups]
    rhs = [from_state[g][0:tg] + from_v[g][0:tg] for g in groups]
    l_all = [jnp.where(strict, m[g][0:tg, 0:tg], 0.0) for g in groups]

    base = SCAN_BASE_BLOCK
    in_base = (rowg // base) == (colg // base)
    eye = jnp.where(rowg == colg, 1.0, 0.0).astype(F32)
    l_base = [jnp.where(in_base, x, 0.0) for x in l_all]
    inv = [eye + x for x in l_base]
    power = l_base
    n_sq = max(1, (base - 1).bit_length()) - 1
    for n in range(n_sq):
        power_s = [_split2(x) for x in power]
        power = [_mm3(x, x) for x in power_s]
        inv = [inv[g] + _mm3(_split2(power[g]), _split2(inv[g])) for g in groups]
    blk_size = base
    while blk_size < t:
        lower_left = jnp.logical_and(
            jnp.logical_and((rowg // (2 * blk_size)) == (colg // (2 * blk_size)), (rowg % (2 * blk_size)) >= blk_size),
            (colg % (2 * blk_size)) < blk_size)
        inv_s = [_split2(x) for x in inv]
        c_a = [_mm3(_split2(jnp.where(lower_left, l_all[g], 0.0)), inv_s[g]) for g in groups]
        inv = [inv[g] + _mm3(inv_s[g], _split2(c_a[g])) for g in groups]
        blk_size *= 2
    sol = [_mm3(_split2(inv[g]), _split2(rhs[g])) for g in groups]

    for g in groups:
        t_rb = jnp.where(incl, m[g][tg:2 * tg, 0:tg], 0.0).astype(BF16)
        y2 = from_state[g][tg:2 * tg] + from_v[g][tg:2 * tg] + _dot(t_rb, sol[g].astype(BF16))
        y = y2[0:t, :]
        for n in range(1, gh):
            y = y + y2[n * t:(n + 1) * t, :]
        y_ref[:, sls[g]] = y
    upd = [_dot(cat0([sol[g], v2[g]]).T.astype(BF16), bk[g]) for g in groups]
    for g in groups:
        state_scr[g] = (states[g] + upd[g]) * p_incl[t - 1:t, sls[g]]


def _rwkv_scan(r, ld, k, v, an, b):
    s, wd = r.shape
    t = min(RWKV_CHUNK, s)
    spec = pl.BlockSpec((t, wd), lambda i: (i, 0))
    return pl.pallas_call(
        _rwkv_scan_kernel,
        grid=(s // t,),
        in_specs=[spec] * 6,
        out_specs=spec,
        out_shape=jax.ShapeDtypeStruct((s, wd), F32),
        scratch_shapes=[pltpu.VMEM((N_HEADS // SCAN_GROUP_HEADS,) + (SCAN_GROUP_HEADS * HEAD_DIM,) * 2, F32)],
        compiler_params=_params(("arbitrary",)),
    )(r, ld, k, v, an, b)


def _merge_kernel(x_ref, oa_ref, ob_ref, y_ref, bonus_ref, g_ref, ga_ref, gb_ref, gc_ref,
                  par_ref, gn_ref, blk_ref, wpa_ref, wpb_ref, wpc_ref, wo_ref, o_ref, *, alpha):
    blk = blk_ref[...]
    y = y_ref[...]
    mean = _dot_exact_rhs(y, blk) * (1.0 / HEAD_DIM)
    yc = y - mean
    var = _head_sums(yc * yc, blk) * (1.0 / HEAD_DIM)
    yn = yc * lax.rsqrt(var + RWKV_GN_EPS) * gn_ref[0:1, :] + gn_ref[1:2, :]
    oc = ((yn + bonus_ref[...]) * g_ref[...]).astype(BF16)

    merged = (_sigmoid(ga_ref[...]) * _dot(oa_ref[...], wpa_ref[...])
              + _sigmoid(gb_ref[...]) * _dot(ob_ref[...], wpb_ref[...])
              + _sigmoid(gc_ref[...]) * _dot(oc, wpc_ref[...]))
    out = _dot(merged.astype(BF16), wo_ref[...])
    z = alpha * x_ref[...] + par_ref[0:1, :] * out
    o_ref[...] = _ln(z) * par_ref[1:2, :] + par_ref[2:3, :]


def _merge(x, oa, ob, y, bonus, g, z_all, par, gn, blk, wpa, wpb, wpc, wo, alpha):
    s, d = x.shape
    wd = RWKV_WIDTH
    tm = min(256, s)
    rows = lambda width, col=0: pl.BlockSpec((tm, width), lambda i, col=col: (i, col))
    full = lambda shape: pl.BlockSpec(shape, lambda i: (0, 0))
    gate0 = COL_GATE // d
    return pl.pallas_call(
        functools.partial(_merge_kernel, alpha=alpha),
        grid=(s // tm,),
        in_specs=[
            rows(d), rows(oa.shape[1]), rows(LRU_WIDTH), rows(wd), rows(wd), rows(wd),
            rows(d, gate0), rows(d, gate0 + 1), rows(d, gate0 + 2),
            full((8, d)), full((8, wd)), full((wd, wd)),
            full(wpa.shape), full((LRU_WIDTH, d)), full((wd, d)), full((d, d)),
        ],
        out_specs=rows(d),
        out_shape=jax.ShapeDtypeStruct((s, d), F32),
        compiler_params=_params(("parallel",)),
    )(x, oa, ob, y, bonus, g, z_all, z_all, z_all, par, gn, blk, wpa, wpb, wpc, wo)


def _block_diag(w):
    nb, n, _ = w.shape
    eye = jnp.eye(nb, dtype=w.dtype)
    return (eye[:, None, :, None] * w[:, :, None, :]).reshape(nb * n, nb * n)


def kernel(x, c, ada_w, ada_b, ln_g, ln_b, ffn_up, ffn_down, w_in, fox_f_bias, lru_conv_w, lru_conv_b, lru_ga_w, lru_ga_b, lru_gx_w, lru_gx_b, lru_lambda, rwkv_mu, rwkv_w0, rwkv_w2, rwkv_a0, rwkv_a2, rwkv_g2, rwkv_k_k, rwkv_k_a, rwkv_r_k, rwkv_gn_w, rwkv_gn_b, w_proj_a, w_proj_b, w_proj_c, w_out):
    batch, s, d = x.shape
    assert batch == 1 and d == 1024
    depth = ada_w.shape[0]
    alpha = float((2 * depth) ** 0.25)
    wd = RWKV_WIDTH

    ada = _ada_all(c, ada_w, ada_b)
    blk = _block_diag(jnp.ones((N_HEADS, HEAD_DIM, HEAD_DIM), BF16))

    o_att = 3 * ATT_WIDTH
    o_lru = o_att + N_HEADS
    o_rwkv = o_lru + 2 * LRU_WIDTH
    o_gate = o_rwkv + 3 * wd + 64 + 64 + 128
    lora = 64

    xs = x[0]
    for l in range(depth):
        mods = [ada[l, n * d:(n + 1) * d] for n in range(9)]
        w_l = w_in[l]

        xs, h_mix = _ffn(xs, _pack_rows(mods[0:3], d),
                         _pack_rows([ln_g[l, 0], ln_b[l, 0], mods[3], mods[4]], d),
                         ffn_up[l, 0].astype(BF16), ffn_down[l, 0].astype(BF16), alpha, True)

        w_c = w_l[:, o_rwkv:o_gate]
        zpad = lambda n: jnp.zeros((d, n), F32)
        w_all = jnp.concatenate([
            w_l[:, 0:o_att],
            w_l[:, o_att:o_lru], zpad(COL_LRU - COL_FL - N_HEADS),
            w_l[:, o_lru:o_rwkv],
            w_c[:, 0:3 * wd],
            w_c[:, 3 * wd:3 * wd + lora], zpad(LORA_PAD - lora),
            w_c[:, 3 * wd + lora:3 * wd + 2 * lora], zpad(LORA_PAD - lora),
            w_c[:, 3 * wd + 2 * lora:], zpad(RWKV_COLS_PAD - 3 * wd - 3 * LORA_PAD),
            w_l[:, o_gate:]], axis=1).astype(BF16)
        assert w_all.shape[1] == N_IN_PAD
        z_all = _in_proj(h_mix, w_all)

        f_bias = _pack_rows([jnp.pad(fox_f_bias[l], (0, LANES - N_HEADS))], LANES)
        qp, kp, vp, fs = _fox_pack(z_all, f_bias)
        o_a = _attention(qp, kp, vp, fs)
        wpa = jnp.pad(w_proj_a[l].reshape(N_HEADS, HEAD_DIM, d), ((0, 0), (0, LANES - HEAD_DIM), (0, 0)))
        wpa = wpa.reshape(N_HEADS * LANES, d).astype(BF16)

        lru_par = _pack_rows([lru_conv_b[l], lru_ga_b[l], lru_gx_b[l], lru_lambda[l]], LRU_WIDTH)
        conv_w = jnp.pad(lru_conv_w[l], ((0, 8 - LRU_CONV), (0, 0)))
        o_b = _lru(z_all, lru_par, conv_w, _block_diag(lru_ga_w[l]).astype(BF16),
                   _block_diag(lru_gx_w[l]).astype(BF16))

        mu = rwkv_mu[l]
        zv = lambda n: jnp.zeros((n,), F32)
        mu_pad = jnp.concatenate([
            mu[0:3 * wd], mu[3 * wd:3 * wd + lora], zv(LORA_PAD - lora),
            mu[3 * wd + lora:3 * wd + 2 * lora], zv(LORA_PAD - lora),
            mu[3 * wd + 2 * lora:], zv(RWKV_COLS_PAD - 3 * wd - 3 * LORA_PAD)])
        rw_par = _pack_rows([rwkv_w0[l], rwkv_a0[l], rwkv_k_k[l], rwkv_k_a[l], rwkv_r_k[l].reshape(wd)], wd)
        pad_rows = lambda w: jnp.pad(w, ((0, LORA_PAD - w.shape[0]), (0, 0)))
        r, ld, kf, v, an, b, bonus, g = _rwkv_prep(
            z_all, _pack_rows([mu_pad], RWKV_COLS_PAD), rw_par,
            pad_rows(rwkv_w2[l]), pad_rows(rwkv_a2[l]), rwkv_g2[l], blk)
        y = _rwkv_scan(r, ld, kf, v, an, b)

        par = _pack_rows([mods[5], ln_g[l, 1], ln_b[l, 1]], d)
        gn = _pack_rows([rwkv_gn_w[l], rwkv_gn_b[l]], wd)
        xs = _merge(xs, o_a, o_b, y, bonus, g, z_all, par, gn, blk,
                    wpa, w_proj_b[l].astype(BF16), w_proj_c[l].astype(BF16),
                    w_out[l].astype(BF16), alpha)

        xs = _ffn(xs, _pack_rows(mods[6:9], d), _pack_rows([ln_g[l, 2], ln_b[l, 2]], d),
                  ffn_up[l, 1].astype(BF16), ffn_down[l, 1].astype(BF16), alpha, False)

    return xs[None]
```

```python
import functools

import jax
import jax.numpy as jnp
from jax import lax
from jax.experimental import pallas as pl
from jax.experimental.pallas import tpu as pltpu

F32 = jnp.float32
BF16 = jnp.bfloat16
HI = lax.Precision.HIGHEST

LANES = 128
HEAD_DIM = 64
N_HEADS = 8
ATT_WIDTH = 512
LRU_WIDTH = 1024
LRU_BLOCK = 64
LRU_CONV = 4
LRU_C = 8.0
RWKV_WIDTH = 512
LORA_PAD = 128
RWKV_COLS_PAD = 2048
RWKV_GN_EPS = 64e-5
LN_EPS = 1e-5
NEG_BIG = -1e30
VMEM_LIMIT = 56 * 1024 * 1024

RWKV_CHUNK = 64
SCAN_BASE_BLOCK = 8
SCAN_GROUP_HEADS = 2
SCAN_CHUNKS_PER_STEP = 4
ATT_TILE = 512
ATT_SLOTS = 8
LOG2E = 1.4426950408889634

COL_QKV = 0
COL_FL = 1536
COL_LRU = 2048
COL_RWKV = 4096
COL_GATE = 6144
N_IN_PAD = 9216

LANE_ONE = HEAD_DIM
LANE_F = HEAD_DIM + 3


def _dot(a, b, prec=None):
    return jnp.dot(a, b, preferred_element_type=F32, precision=prec)


def _dot_nt(a, b, prec=None):
    return lax.dot_general(a, b, (((1,), (1,)), ((), ())), preferred_element_type=F32, precision=prec)


def _split2(x):
    hi = x.astype(BF16)
    return hi, (x - hi.astype(F32)).astype(BF16)


def _split3(x):
    hi = x.astype(BF16)
    r = x - hi.astype(F32)
    mid = r.astype(BF16)
    return hi, mid, (r - mid.astype(F32)).astype(BF16)


def _mm3(a, b):
    cross = _dot(jnp.concatenate([a[0], a[1]], axis=1), jnp.concatenate([b[1], b[0]], axis=0))
    return _dot(a[0], b[0]) + cross


def _mm3_nt(a, b):
    cross = _dot_nt(jnp.concatenate([a[0], a[1]], axis=1), jnp.concatenate([b[1], b[0]], axis=1))
    return _dot_nt(a[0], b[0]) + cross


def _dot_x3(a, b):
    return _mm3(_split2(a), _split2(b))


def _dot_exact_lhs(sel, x):
    hi, mid, lo = _split3(x)
    return _dot(sel, hi) + (_dot(sel, mid) + _dot(sel, lo))


def _dot_exact_rhs(x, sel):
    hi, mid, lo = _split3(x)
    return _dot(hi, sel) + (_dot(mid, sel) + _dot(lo, sel))


def _head_sums(x, blk):
    hi, lo = _split2(x)
    return _dot(hi, blk) + _dot(lo, blk)


def _ln(x):
    mu = jnp.mean(x, axis=-1, keepdims=True)
    xc = x - mu
    var = jnp.mean(xc * xc, axis=-1, keepdims=True)
    return xc * lax.rsqrt(var + LN_EPS)


def _softplus(x):
    return jnp.maximum(x, 0.0) + jnp.log1p(jnp.exp(-jnp.abs(x)))


def _sigmoid(x):
    return 1.0 / (1.0 + jnp.exp(-x))


def _params(sem):
    return pltpu.CompilerParams(dimension_semantics=sem, vmem_limit_bytes=VMEM_LIMIT)


def _pack_rows(rows, width):
    rows = [r.reshape(1, width).astype(F32) for r in rows]
    pad = jnp.zeros((8 - len(rows), width), F32)
    return jnp.concatenate(rows + [pad], axis=0)


def _ada_kernel(c_ref, w_ref, b_ref, o_ref):
    c = c_ref[...]
    ca = c * _sigmoid(c)
    o_ref[0] = _dot(ca, w_ref[0], HI) + b_ref[0]


def _ada_all(c, ada_w, ada_b):
    n_layers, d, n = ada_w.shape
    tn = 1152
    c8 = jnp.broadcast_to(c.astype(F32), (8, d))
    out = pl.pallas_call(
        _ada_kernel,
        grid=(n_layers, n // tn),
        in_specs=[
            pl.BlockSpec((8, d), lambda l, j: (0, 0)),
            pl.BlockSpec((1, d, tn), lambda l, j: (l, 0, j)),
            pl.BlockSpec((1, 1, tn), lambda l, j: (l, 0, j)),
        ],
        out_specs=pl.BlockSpec((1, 8, tn), lambda l, j: (l, 0, j)),
        out_shape=jax.ShapeDtypeStruct((n_layers, 8, n), F32),
        compiler_params=_params(("arbitrary", "arbitrary")),
    )(c8, ada_w, ada_b.reshape(n_layers, 1, n))
    return out[:, 0, :]


def _ffn_kernel(x_ref, mod_ref, lnp_ref, wup_ref, wd_ref, o_ref, *next_ref, alpha, tf):
    d_ff = wd_ref.shape[0]
    n_chunks = d_ff // tf
    x = x_ref[...]
    h = (_ln(x) * (1.0 + mod_ref[1:2, :]) + mod_ref[0:1, :]).astype(BF16)

    def up(c):
        u = _dot(h, wup_ref[:, c * tf:(c + 1) * tf])
        g = _dot(h, wup_ref[:, d_ff + c * tf:d_ff + (c + 1) * tf])
        return u, g

    acc = None
    nxt = up(0)
    for c in range(n_chunks):
        u, g = nxt
        if c + 1 < n_chunks:
            nxt = up(c + 1)
        act = (u * _sigmoid(u) * g).astype(BF16)
        down = _dot(act, wd_ref[c * tf:(c + 1) * tf, :])
        acc = down if acc is None else acc + down

    z = alpha * x + (0.5 * mod_ref[2:3, :]) * acc
    out = _ln(z) * lnp_ref[0:1, :] + lnp_ref[1:2, :]
    o_ref[...] = out
    if next_ref:
        next_ref[0][...] = (_ln(out) * (1.0 + lnp_ref[3:4, :]) + lnp_ref[2:3, :]).astype(BF16)


def _ffn(x, mod, lnp, w_up, w_down, alpha, emit_next):
    s, d = x.shape
    d_ff = w_down.shape[0]
    tm = min(512, s)
    rows = pl.BlockSpec((tm, d), lambda i: (i, 0))
    out_specs, out_shape = rows, jax.ShapeDtypeStruct((s, d), F32)
    if emit_next:
        out_specs, out_shape = [rows, rows], [out_shape, jax.ShapeDtypeStruct((s, d), BF16)]
    return pl.pallas_call(
        functools.partial(_ffn_kernel, alpha=alpha, tf=256),
        grid=(s // tm,),
        in_specs=[
            rows,
            pl.BlockSpec((8, d), lambda i: (0, 0)),
            pl.BlockSpec((8, d), lambda i: (0, 0)),
            pl.BlockSpec((d, 2 * d_ff), lambda i: (0, 0)),
            pl.BlockSpec((d_ff, d), lambda i: (0, 0)),
        ],
        out_specs=out_specs,
        out_shape=out_shape,
        compiler_params=_params(("parallel",)),
    )(x, mod, lnp, w_up, w_down)


def _in_proj_kernel(h_ref, w_ref, o_ref):
    o_ref[...] = _dot(h_ref[...], w_ref[...])


def _in_proj(h, w):
    s, d = h.shape
    n = w.shape[1]
    tm = min(2048, s)
    tn = 512
    return pl.pallas_call(
        _in_proj_kernel,
        grid=(s // tm, n // tn),
        in_specs=[
            pl.BlockSpec((tm, d), lambda i, j: (i, 0)),
            pl.BlockSpec((d, tn), lambda i, j: (0, j)),
        ],
        out_specs=pl.BlockSpec((tm, tn), lambda i, j: (i, j)),
        out_shape=jax.ShapeDtypeStruct((s, n), F32),
        compiler_params=_params(("parallel", "parallel")),
    )(h, w)


def _fox_pack_kernel(qkv_ref, fl_ref, bias_ref, qp_ref, kp_ref, vp_ref, fs_ref, carry):
    tm = fl_ref.shape[0]

    @pl.when(pl.program_id(0) == 0)
    def _():
        carry[...] = jnp.zeros_like(carry)

    log_f = -_softplus(-(fl_ref[...] + bias_ref[0:1, :]))
    row = lax.broadcasted_iota(jnp.int32, (tm, tm), 0)
    col = lax.broadcasted_iota(jnp.int32, (tm, tm), 1)
    tril = jnp.where(col <= row, 1.0, 0.0).astype(BF16)
    f_loc = _dot_exact_lhs(tril, log_f * LOG2E)

    start = carry[...]
    r8 = lax.broadcasted_iota(jnp.int32, (8, LANES), 0)
    c8 = lax.broadcasted_iota(jnp.int32, (8, LANES), 1)
    diag = jnp.where(r8 == c8, start, 0.0)
    fs_ref[0] = _dot_exact_rhs(diag, jnp.ones((LANES, LANES), BF16))
    carry[...] = start + f_loc[tm - 1:tm, :]

    lane = lax.broadcasted_iota(jnp.int32, (1, LANES), 1)
    head_lanes = lane < HEAD_DIM
    ones3 = jnp.logical_and(lane >= LANE_ONE, lane < LANE_ONE + 3).astype(F32)
    onesf = jnp.logical_and(lane >= LANE_F, lane < LANE_F + 3).astype(F32)
    one1 = (lane == LANE_ONE).astype(F32)
    for h in range(N_HEADS):
        f = jnp.broadcast_to(f_loc[:, h:h + 1], (tm, LANES))
        pieces = [p.astype(F32) for p in _split3(f)]
        q_extra = ones3
        k_extra = onesf
        for n, piece in enumerate(pieces):
            q_extra = q_extra + jnp.where(lane == LANE_F + n, piece, 0.0)
            k_extra = k_extra - jnp.where(lane == LANE_ONE + n, piece, 0.0)
        pair = h // 2
        heads = []
        for base in (0, ATT_WIDTH, 2 * ATT_WIDTH):
            xh = qkv_ref[:, base + pair * LANES:base + (pair + 1) * LANES]
            heads.append(pltpu.roll(xh, HEAD_DIM, axis=1) if h % 2 else xh)
        sl = slice(h * LANES, (h + 1) * LANES)
        qp_ref[:, sl] = jnp.where(head_lanes, heads[0] * (LOG2E * HEAD_DIM ** -0.5), q_extra).astype(BF16)
        kp_ref[:, sl] = jnp.where(head_lanes, heads[1], k_extra).astype(BF16)
        vp_ref[:, sl] = jnp.where(head_lanes, heads[2], one1).astype(BF16)


def _fox_pack(z_all, bias):
    s = z_all.shape[0]
    tm = min(ATT_TILE, s)
    wide = N_HEADS * LANES
    packed = jax.ShapeDtypeStruct((s, wide), BF16)
    return pl.pallas_call(
        _fox_pack_kernel,
        grid=(s // tm,),
        in_specs=[
            pl.BlockSpec((tm, 3 * ATT_WIDTH), lambda i: (i, COL_QKV // (3 * ATT_WIDTH))),
            pl.BlockSpec((tm, LANES), lambda i: (i, COL_FL // LANES)),
            pl.BlockSpec((8, LANES), lambda i: (0, 0)),
        ],
        out_specs=[pl.BlockSpec((tm, wide), lambda i: (i, 0))] * 3
        + [pl.BlockSpec((1, 8, LANES), lambda i: (i, 0, 0))],
        out_shape=[packed] * 3 + [jax.ShapeDtypeStruct((s // tm, 8, LANES), F32)],
        scratch_shapes=[pltpu.VMEM((8, LANES), F32)],
        compiler_params=_params(("arbitrary",)),
    )(z_all, z_all, bias)


def _attn_kernel(q_ref, k_ref, v_ref, fs_ref, o_ref, m_scr, acc_scr, *, tile):
    h = pl.program_id(0)
    i = pl.program_id(1)
    q = q_ref[...]
    reps = tile // LANES
    row = lax.broadcasted_iota(jnp.int32, (tile, tile), 0)
    col = lax.broadcasted_iota(jnp.int32, (tile, tile), 1)
    fs_q = fs_ref[i, pl.ds(h, 1), :]

    m_scr[...] = jnp.full(m_scr.shape, NEG_BIG, F32)
    acc_scr[...] = jnp.zeros_like(acc_scr)

    def run(tiles):
        offs = [pl.multiple_of(j * tile, tile) for j, _, _ in tiles]
        m_old = [m_scr[slot] for _, slot, _ in tiles]
        acc_old = [acc_scr[slot] for _, slot, _ in tiles]
        shift = [fs_q - fs_ref[j, pl.ds(h, 1), :] for j, _, _ in tiles]
        s = [_dot_nt(q, k_ref[pl.ds(off, tile), :]) for off in offs]
        s = [jnp.where(col <= row, x, NEG_BIG) if masked else x for x, (_, _, masked) in zip(s, tiles)]
        part = []
        for x in s:
            pm = x[:, 0:LANES]
            for n in range(1, reps):
                pm = jnp.maximum(pm, x[:, n * LANES:(n + 1) * LANES])
            part.append(pm)
        m_new = [jnp.maximum(mo, jnp.max(pm, axis=-1, keepdims=True) + sh)
                 for mo, pm, sh in zip(m_old, part, shift)]
        p = [jnp.exp2(x - pltpu.repeat(mn - sh, reps, axis=1)).astype(BF16) for x, mn, sh in zip(s, m_new, shift)]
        pv = [_dot(x, v_ref[pl.ds(off, tile), :]) for x, off in zip(p, offs)]
        for n, (_, slot, _) in enumerate(tiles):
            acc_scr[slot] = jnp.exp2(m_old[n] - m_new[n]) * acc_old[n] + pv[n]
            m_scr[slot] = m_new[n]

    def body(jj, carry):
        run([(ATT_SLOTS * jj + n, n, False) for n in range(ATT_SLOTS)])
        return carry

    groups = i // ATT_SLOTS
    lax.fori_loop(0, groups, body, 0)

    for rem in range(ATT_SLOTS):
        @pl.when(i % ATT_SLOTS == rem)
        def _(rem=rem):
            run([(ATT_SLOTS * groups + n, n, False) for n in range(rem)] + [(i, rem, True)])

    m = m_scr[0]
    for n in range(1, ATT_SLOTS):
        m = jnp.maximum(m, m_scr[n])
    acc = jnp.exp2(m_scr[0] - m) * acc_scr[0]
    for n in range(1, ATT_SLOTS):
        acc = acc + jnp.exp2(m_scr[n] - m) * acc_scr[n]
    lane = lax.broadcasted_iota(jnp.int32, (1, LANES), 1)
    o = jnp.where(lane < HEAD_DIM, acc / acc[:, LANE_ONE:LANE_ONE + 1], 0.0)
    o_ref[...] = o.astype(o_ref.dtype)


def _attention(qp, kp, vp, fs):
    s, wide = qp.shape
    tile = min(ATT_TILE, s)
    return pl.pallas_call(
        functools.partial(_attn_kernel, tile=tile),
        grid=(N_HEADS, s // tile),
        in_specs=[
            pl.BlockSpec((tile, LANES), lambda h, i: (i, h)),
            pl.BlockSpec((s, LANES), lambda h, i: (0, h)),
            pl.BlockSpec((s, LANES), lambda h, i: (0, h)),
            pl.BlockSpec(fs.shape, lambda h, i: (0, 0, 0)),
        ],
        out_specs=pl.BlockSpec((tile, LANES), lambda h, i: (i, h)),
        out_shape=jax.ShapeDtypeStruct((s, wide), BF16),
        scratch_shapes=[pltpu.VMEM((ATT_SLOTS, tile, LANES), F32), pltpu.VMEM((ATT_SLOTS, tile, LANES), F32)],
        compiler_params=_params(("parallel", "arbitrary")),
    )(qp, kp, vp, fs)


def _lru_kernel(xb_ref, yb_ref, par_ref, cw_ref, ga_ref, gx_ref, o_ref,
                xpad_scr, a_scr, u_scr, h_scr, carry_scr):
    tm, w = xb_ref.shape

    @pl.when(pl.program_id(0) == 0)
    def _():
        xpad_scr[0:8, :] = jnp.zeros((8, w), F32)
        carry_scr[...] = jnp.zeros_like(carry_scr)

    xb = xb_ref[...]
    xpad_scr[8:8 + tm, :] = xb
    xc = xb * cw_ref[LRU_CONV - 1:LRU_CONV, :] + par_ref[0:1, :]
    for d in range(1, LRU_CONV):
        xc = xc + xpad_scr[8 - d:8 - d + tm, :] * cw_ref[LRU_CONV - 1 - d:LRU_CONV - d, :]
    xpad_scr[0:8, :] = xb[tm - 8:tm, :]

    xcb = xc.astype(BF16)
    r = _sigmoid(_dot(xcb, ga_ref[...]) + par_ref[1:2, :])
    gi = _sigmoid(_dot(xcb, gx_ref[...]) + par_ref[2:3, :])
    log_a = (-LRU_C) * r * _softplus(-par_ref[3:4, :])
    a = jnp.exp(log_a)
    u = jnp.sqrt(-jnp.tanh(log_a) * (a * a + 1.0)) * (gi * xc)

    r8 = lax.broadcasted_iota(jnp.int32, (tm, w), 0) % 8
    for d in (1, 2, 4):
        keep = r8 >= d
        a_prev = jnp.where(keep, pltpu.roll(a, d, axis=0), 1.0)
        u_prev = jnp.where(keep, pltpu.roll(u, d, axis=0), 0.0)
        u = a * u_prev + u
        a = a * a_prev
    a_scr[...] = a
    u_scr[...] = u

    def slab(k, carry):
        off = pl.multiple_of(k * 8, 8)
        h8 = a_scr[pl.ds(off, 8), :] * carry + u_scr[pl.ds(off, 8), :]
        h_scr[pl.ds(off, 8), :] = h8
        return jnp.broadcast_to(h8[7:8, :], (8, w))

    carry_scr[...] = lax.fori_loop(0, tm // 8, slab, carry_scr[...], unroll=4)

    yb = yb_ref[...]
    gelu = 0.5 * yb * (1.0 + jnp.tanh(0.7978845608028654 * (yb + 0.044715 * (yb * yb * yb))))
    o_ref[...] = (h_scr[...] * gelu).astype(o_ref.dtype)


def _lru(z_lru, par, conv_w, ga, gx):
    s = z_lru.shape[0]
    w = LRU_WIDTH
    tm = min(512, s)
    return pl.pallas_call(
        _lru_kernel,
        grid=(s // tm,),
        in_specs=[
            pl.BlockSpec((tm, w), lambda i: (i, COL_LRU // w)),
            pl.BlockSpec((tm, w), lambda i: (i, COL_LRU // w + 1)),
            pl.BlockSpec((8, w), lambda i: (0, 0)),
            pl.BlockSpec((8, w), lambda i: (0, 0)),
            pl.BlockSpec((w, w), lambda i: (0, 0)),
            pl.BlockSpec((w, w), lambda i: (0, 0)),
        ],
        out_specs=pl.BlockSpec((tm, w), lambda i: (i, 0)),
        out_shape=jax.ShapeDtypeStruct((s, w), BF16),
        scratch_shapes=[
            pltpu.VMEM((tm + 8, w), F32),
            pltpu.VMEM((tm, w), F32),
            pltpu.VMEM((tm, w), F32),
            pltpu.VMEM((tm, w), F32),
            pltpu.VMEM((8, w), F32),
        ],
        compiler_params=_params(("arbitrary",)),
    )(z_lru, z_lru, par, conv_w, ga, gx)


def _rwkv_prep_kernel(z_ref, mu_ref, par_ref, w2_ref, a2_ref, g2_ref, blk_ref,
                      r_ref, ld_ref, k_ref, v_ref, an_ref, b_ref, bonus_ref, g_ref, prev_scr):
    tm = z_ref.shape[0]
    wd = RWKV_WIDTH

    @pl.when(pl.program_id(0) == 0)
    def _():
        prev_scr[...] = jnp.zeros_like(prev_scr)

    z = z_ref[...]
    row = lax.broadcasted_iota(jnp.int32, z.shape, 0)
    z_prev = jnp.where(row == 0, prev_scr[0:1, :], pltpu.roll(z, 1, axis=0))
    prev_scr[...] = jnp.broadcast_to(z[tm - 1:tm, :], prev_scr.shape)
    zs = z + (z_prev - z) * mu_ref[0:1, :]

    r = zs[:, 0:wd]
    k = zs[:, wd:2 * wd]
    v = zs[:, 2 * wd:3 * wd]
    wl = zs[:, 3 * wd:3 * wd + LORA_PAD]
    al = zs[:, 3 * wd + LORA_PAD:3 * wd + 2 * LORA_PAD]
    gl = zs[:, 3 * wd + 2 * LORA_PAD:3 * wd + 3 * LORA_PAD]
    w0, a0, k_k, k_a, r_k = (par_ref[n:n + 1, :] for n in range(5))

    w = -_softplus(-(w0 + _dot_x3(jnp.tanh(wl), w2_ref[...]))) - 0.5
    a = _sigmoid(a0 + _dot_x3(al, a2_ref[...]))
    g = _dot_x3(_sigmoid(gl), g2_ref[...])

    blk = blk_ref[...]
    kk = k * k_k
    norm = jnp.sqrt(_head_sums(kk * kk, blk))
    kk = kk / jnp.maximum(norm, 1e-12)
    kf = k * (1.0 + (a - 1.0) * k_a)

    r_ref[...] = r
    ld_ref[...] = -jnp.exp(w)
    k_ref[...] = kf
    v_ref[...] = v
    an_ref[...] = -kk
    b_ref[...] = kk * a
    bonus_ref[...] = _head_sums(r * kf * r_k, blk) * v
    g_ref[...] = g


def _rwkv_prep(zc, mu, par, w2, a2, g2, blk):
    s = zc.shape[0]
    n = RWKV_COLS_PAD
    wd = RWKV_WIDTH
    tm = min(512, s)
    full = lambda shape: pl.BlockSpec(shape, lambda i: (0, 0))
    out = jax.ShapeDtypeStruct((s, wd), F32)
    return pl.pallas_call(
        _rwkv_prep_kernel,
        grid=(s // tm,),
        in_specs=[
            pl.BlockSpec((tm, n), lambda i: (i, COL_RWKV // n)),
            full((8, n)), full((8, wd)), full((LORA_PAD, wd)), full((LORA_PAD, wd)),
            full((LORA_PAD, wd)), full((wd, wd)),
        ],
        out_specs=[pl.BlockSpec((tm, wd), lambda i: (i, 0))] * 8,
        out_shape=[out] * 8,
        scratch_shapes=[pltpu.VMEM((8, n), F32)],
        compiler_params=_params(("arbitrary",)),
    )(zc, mu, par, w2, a2, g2, blk)


def _rwkv_scan_kernel(r_ref, ld_ref, k_ref, v_ref, an_ref, b_ref, y_ref, state_scr):
    t = RWKV_CHUNK
    n_chunks = r_ref.shape[0] // t

    @pl.when(pl.program_id(0) == 0)
    def _():
        state_scr[...] = jnp.zeros_like(state_scr)

    gh = SCAN_GROUP_HEADS
    gw = gh * HEAD_DIM
    tg = gh * t
    groups = range(N_HEADS // gh)
    chunks = range(n_chunks)
    units = [(c, g) for c in chunks for g in groups]
    sls = [slice(g * gw, (g + 1) * gw) for g in groups]
    cat0 = lambda xs: jnp.concatenate(xs, axis=0)

    lane = lax.broadcasted_iota(jnp.int32, (1, gw), 1)
    head_masks = [jnp.logical_and(lane >= n * HEAD_DIM, lane < (n + 1) * HEAD_DIM) for n in range(gh)]
    rowg = lax.broadcasted_iota(jnp.int32, (tg, tg), 0)
    colg = lax.broadcasted_iota(jnp.int32, (tg, tg), 1)
    same_head = (rowg // t) == (colg // t)
    strict = jnp.logical_and(same_head, (colg % t) < (rowg % t))
    incl = jnp.logical_and(same_head, (colg % t) <= (rowg % t))

    def stack(x):
        return jnp.concatenate([jnp.where(mask, x, 0.0) for mask in head_masks], axis=0)

    row3 = lax.broadcasted_iota(jnp.int32, (t, 3 * t), 0)
    col3 = lax.broadcasted_iota(jnp.int32, (t, 3 * t), 1)
    tril3 = jnp.where(col3 % t <= row3, 1.0, 0.0).astype(BF16)
    rows = [slice(c * t, (c + 1) * t) for c in chunks]
    ld = [ld_ref[rs, :] for rs in rows]
    c_incl = [_dot(tril3, jnp.concatenate(_split3(x), axis=0)) for x in ld]
    p_incl = [jnp.exp(x) for x in c_incl]
    p_inv = [jnp.exp(-x) for x in c_incl]
    r_s = [r_ref[rows[c], :] * p_incl[c] for c in chunks]
    a_s = [an_ref[rows[c], :] * jnp.exp(c_incl[c] - ld[c]) for c in chunks]
    b_s = [b_ref[rows[c], :] * p_inv[c] for c in chunks]
    k_s = [k_ref[rows[c], :] * p_inv[c] for c in chunks]

    ar = {(c, g): cat0([stack(a_s[c][:, sls[g]]), stack(r_s[c][:, sls[g]])]).astype(BF16) for c, g in units}
    v2 = {(c, g): stack(v_ref[rows[c], sls[g]]) for c, g in units}
    bk = {(c, g): cat0([stack(b_s[c][:, sls[g]]), stack(k_s[c][:, sls[g]])]).astype(BF16) for c, g in units}
    m = {u: _dot_nt(ar[u], bk[u]) for u in units}
    lt_k = {u: cat0([jnp.where(strict, m[u][0:tg, tg:2 * tg], 0.0),
                     jnp.where(incl, m[u][tg:2 * tg, tg:2 * tg], 0.0)]).astype(BF16) for u in units}
    from_v = {u: _dot(lt_k[u], v2[u].astype(BF16)) for u in units}
    t_rb = {u: jnp.where(incl, m[u][tg:2 * tg, 0:tg], 0.0).astype(BF16) for u in units}
    l_all = {u: jnp.where(strict, m[u][0:tg, 0:tg], 0.0) for u in units}

    base = SCAN_BASE_BLOCK
    in_base = (rowg // base) == (colg // base)
    eye = jnp.where(rowg == colg, 1.0, 0.0).astype(F32)
    power = {u: jnp.where(in_base, l_all[u], 0.0) for u in units}
    inv = {u: eye + power[u] for u in units}
    for _ in range(max(1, (base - 1).bit_length()) - 1):
        power_s = {u: _split2(power[u]) for u in units}
        power = {u: _mm3(power_s[u], power_s[u]) for u in units}
        inv = {u: inv[u] + _mm3(_split2(power[u]), _split2(inv[u])) for u in units}
    blk_size = base
    while blk_size < t:
        lower_left = jnp.logical_and(
            jnp.logical_and((rowg // (2 * blk_size)) == (colg // (2 * blk_size)), (rowg % (2 * blk_size)) >= blk_size),
            (colg % (2 * blk_size)) < blk_size)
        inv_s = {u: _split2(inv[u]) for u in units}
        c_a = {u: _mm3(_split2(jnp.where(lower_left, l_all[u], 0.0)), inv_s[u]) for u in units}
        inv = {u: inv[u] + _mm3(inv_s[u], _split2(c_a[u])) for u in units}
        blk_size *= 2
    inv_s = {u: _split2(inv[u]) for u in units}

    states = [state_scr[g] for g in groups]
    for c in chunks:
        from_state = [_dot_nt(ar[c, g], states[g].astype(BF16)) for g in groups]
        sol = [_mm3(inv_s[c, g], _split2(from_state[g][0:tg] + from_v[c, g][0:tg])) for g in groups]
        for g in groups:
            y2 = from_state[g][tg:2 * tg] + from_v[c, g][tg:2 * tg] + _dot(t_rb[c, g], sol[g].astype(BF16))
            y = y2[0:t, :]
            for n in range(1, gh):
                y = y + y2[n * t:(n + 1) * t, :]
            y_ref[rows[c], sls[g]] = y
        upd = [_dot(cat0([sol[g], v2[c, g]]).T.astype(BF16), bk[c, g]) for g in groups]
        states = [(states[g] + upd[g]) * p_incl[c][t - 1:t, sls[g]] for g in groups]
    for g in groups:
        state_scr[g] = states[g]


def _rwkv_scan(r, ld, k, v, an, b):
    s, wd = r.shape
    rows = min(RWKV_CHUNK * SCAN_CHUNKS_PER_STEP, s)
    spec = pl.BlockSpec((rows, wd), lambda i: (i, 0))
    return pl.pallas_call(
        _rwkv_scan_kernel,
        grid=(s // rows,),
        in_specs=[spec] * 6,
        out_specs=spec,
        out_shape=jax.ShapeDtypeStruct((s, wd), F32),
        scratch_shapes=[pltpu.VMEM((N_HEADS // SCAN_GROUP_HEADS,) + (SCAN_GROUP_HEADS * HEAD_DIM,) * 2, F32)],
        compiler_params=_params(("arbitrary",)),
    )(r, ld, k, v, an, b)


def _merge_kernel(x_ref, oa_ref, ob_ref, y_ref, bonus_ref, g_ref, ga_ref, gb_ref, gc_ref,
                  par_ref, gn_ref, blk_ref, wpa_ref, wpb_ref, wpc_ref, wo_ref, o_ref, *, alpha):
    blk = blk_ref[...]
    y = y_ref[...]
    mean = _dot_exact_rhs(y, blk) * (1.0 / HEAD_DIM)
    yc = y - mean
    var = _head_sums(yc * yc, blk) * (1.0 / HEAD_DIM)
    yn = yc * lax.rsqrt(var + RWKV_GN_EPS) * gn_ref[0:1, :] + gn_ref[1:2, :]
    oc = ((yn + bonus_ref[...]) * g_ref[...]).astype(BF16)

    merged = (_sigmoid(ga_ref[...]) * _dot(oa_ref[...], wpa_ref[...])
              + _sigmoid(gb_ref[...]) * _dot(ob_ref[...], wpb_ref[...])
              + _sigmoid(gc_ref[...]) * _dot(oc, wpc_ref[...]))
    out = _dot(merged.astype(BF16), wo_ref[...])
    z = alpha * x_ref[...] + par_ref[0:1, :] * out
    o_ref[...] = _ln(z) * par_ref[1:2, :] + par_ref[2:3, :]


def _merge(x, oa, ob, y, bonus, g, z_all, par, gn, blk, wpa, wpb, wpc, wo, alpha):
    s, d = x.shape
    wd = RWKV_WIDTH
    tm = min(512, s)
    rows = lambda width, col=0: pl.BlockSpec((tm, width), lambda i, col=col: (i, col))
    full = lambda shape: pl.BlockSpec(shape, lambda i: (0, 0))
    gate0 = COL_GATE // d
    return pl.pallas_call(
        functools.partial(_merge_kernel, alpha=alpha),
        grid=(s // tm,),
        in_specs=[
            rows(d), rows(oa.shape[1]), rows(LRU_WIDTH), rows(wd), rows(wd), rows(wd),
            rows(d, gate0), rows(d, gate0 + 1), rows(d, gate0 + 2),
            full((8, d)), full((8, wd)), full((wd, wd)),
            full(wpa.shape), full((LRU_WIDTH, d)), full((wd, d)), full((d, d)),
        ],
        out_specs=rows(d),
        out_shape=jax.ShapeDtypeStruct((s, d), F32),
        compiler_params=_params(("parallel",)),
    )(x, oa, ob, y, bonus, g, z_all, z_all, z_all, par, gn, blk, wpa, wpb, wpc, wo)


def _block_diag(w):
    nb, n, _ = w.shape
    eye = jnp.eye(nb, dtype=w.dtype)
    return (eye[:, None, :, None] * w[:, :, None, :]).reshape(nb * n, nb * n)


def kernel(x, c, ada_w, ada_b, ln_g, ln_b, ffn_up, ffn_down, w_in, fox_f_bias, lru_conv_w, lru_conv_b, lru_ga_w, lru_ga_b, lru_gx_w, lru_gx_b, lru_lambda, rwkv_mu, rwkv_w0, rwkv_w2, rwkv_a0, rwkv_a2, rwkv_g2, rwkv_k_k, rwkv_k_a, rwkv_r_k, rwkv_gn_w, rwkv_gn_b, w_proj_a, w_proj_b, w_proj_c, w_out):
    batch, s, d = x.shape
    assert batch == 1 and d == 1024
    depth = ada_w.shape[0]
    alpha = float((2 * depth) ** 0.25)
    wd = RWKV_WIDTH

    ada = _ada_all(c, ada_w, ada_b)
    blk = _block_diag(jnp.ones((N_HEADS, HEAD_DIM, HEAD_DIM), BF16))

    o_att = 3 * ATT_WIDTH
    o_lru = o_att + N_HEADS
    o_rwkv = o_lru + 2 * LRU_WIDTH
    o_gate = o_rwkv + 3 * wd + 64 + 64 + 128
    lora = 64

    xs = x[0]
    for l in range(depth):
        mods = [ada[l, n * d:(n + 1) * d] for n in range(9)]
        w_l = w_in[l]

        xs, h_mix = _ffn(xs, _pack_rows(mods[0:3], d),
                         _pack_rows([ln_g[l, 0], ln_b[l, 0], mods[3], mods[4]], d),
                         ffn_up[l, 0].astype(BF16), ffn_down[l, 0].astype(BF16), alpha, True)

        w_c = w_l[:, o_rwkv:o_gate]
        zpad = lambda n: jnp.zeros((d, n), F32)
        w_all = jnp.concatenate([
            w_l[:, 0:o_att],
            w_l[:, o_att:o_lru], zpad(COL_LRU - COL_FL - N_HEADS),
            w_l[:, o_lru:o_rwkv],
            w_c[:, 0:3 * wd],
            w_c[:, 3 * wd:3 * wd + lora], zpad(LORA_PAD - lora),
            w_c[:, 3 * wd + lora:3 * wd + 2 * lora], zpad(LORA_PAD - lora),
            w_c[:, 3 * wd + 2 * lora:], zpad(RWKV_COLS_PAD - 3 * wd - 3 * LORA_PAD),
            w_l[:, o_gate:]], axis=1).astype(BF16)
        assert w_all.shape[1] == N_IN_PAD
        z_all = _in_proj(h_mix, w_all)

        f_bias = _pack_rows([jnp.pad(fox_f_bias[l], (0, LANES - N_HEADS))], LANES)
        qp, kp, vp, fs = _fox_pack(z_all, f_bias)
        o_a = _attention(qp, kp, vp, fs)
        wpa = jnp.pad(w_proj_a[l].reshape(N_HEADS, HEAD_DIM, d), ((0, 0), (0, LANES - HEAD_DIM), (0, 0)))
        wpa = wpa.reshape(N_HEADS * LANES, d).astype(BF16)

        lru_par = _pack_rows([lru_conv_b[l], lru_ga_b[l], lru_gx_b[l], lru_lambda[l]], LRU_WIDTH)
        conv_w = jnp.pad(lru_conv_w[l], ((0, 8 - LRU_CONV), (0, 0)))
        o_b = _lru(z_all, lru_par, conv_w, _block_diag(lru_ga_w[l]).astype(BF16),
                   _block_diag(lru_gx_w[l]).astype(BF16))

        mu = rwkv_mu[l]
        zv = lambda n: jnp.zeros((n,), F32)
        mu_pad = jnp.concatenate([
            mu[0:3 * wd], mu[3 * wd:3 * wd + lora], zv(LORA_PAD - lora),
            mu[3 * wd + lora:3 * wd + 2 * lora], zv(LORA_PAD - lora),
            mu[3 * wd + 2 * lora:], zv(RWKV_COLS_PAD - 3 * wd - 3 * LORA_PAD)])
        rw_par = _pack_rows([rwkv_w0[l], rwkv_a0[l], rwkv_k_k[l], rwkv_k_a[l], rwkv_r_k[l].reshape(wd)], wd)
        pad_rows = lambda w: jnp.pad(w, ((0, LORA_PAD - w.shape[0]), (0, 0)))
        r, ld, kf, v, an, b, bonus, g = _rwkv_prep(
            z_all, _pack_rows([mu_pad], RWKV_COLS_PAD), rw_par,
            pad_rows(rwkv_w2[l]), pad_rows(rwkv_a2[l]), rwkv_g2[l], blk)
        y = _rwkv_scan(r, ld, kf, v, an, b)

        par = _pack_rows([mods[5], ln_g[l, 1], ln_b[l, 1]], d)
        gn = _pack_rows([rwkv_gn_w[l], rwkv_gn_b[l]], wd)
        xs = _merge(xs, o_a, o_b, y, bonus, g, z_all, par, gn, blk,
                    wpa, w_proj_b[l].astype(BF16), w_proj_c[l].astype(BF16),
                    w_out[l].astype(BF16), alpha)

        xs = _ffn(xs, _pack_rows(mods[6:9], d), _pack_rows([ln_g[l, 2], ln_b[l, 2]], d),
                  ffn_up[l, 1].astype(BF16), ffn_down[l, 1].astype(BF16), alpha, False)

    return xs[None]
```

```python
import functools

import jax
import jax.numpy as jnp
from jax import lax
from jax.experimental import pallas as pl
from jax.experimental.pallas import tpu as pltpu

F32 = jnp.float32
BF16 = jnp.bfloat16
HI = lax.Precision.HIGHEST

LANES = 128
HEAD_DIM = 64
N_HEADS = 8
ATT_WIDTH = 512
LRU_WIDTH = 1024
LRU_BLOCK = 64
LRU_CONV = 4
LRU_C = 8.0
RWKV_WIDTH = 512
LORA_PAD = 128
RWKV_COLS_PAD = 2048
RWKV_GN_EPS = 64e-5
LN_EPS = 1e-5
NEG_BIG = -1e30
VMEM_LIMIT = 56 * 1024 * 1024

RWKV_CHUNK = 64
SCAN_BASE_BLOCK = 8
SCAN_GROUP_HEADS = 2
SCAN_CHUNKS_PER_STEP = 4
ATT_TILE = 512
ATT_SLOTS = 8
LOG2E = 1.4426950408889634

COL_QKV = 0
COL_FL = 1536
COL_LRU = 2048
COL_RWKV = 4096
COL_GATE = 6144
N_IN_PAD = 9216

LANE_ONE = HEAD_DIM
LANE_F = HEAD_DIM + 3


def _dot(a, b, prec=None):
    return jnp.dot(a, b, preferred_element_type=F32, precision=prec)


def _dot_nt(a, b, prec=None):
    return lax.dot_general(a, b, (((1,), (1,)), ((), ())), preferred_element_type=F32, precision=prec)


def _split2(x):
    hi = x.astype(BF16)
    return hi, (x - hi.astype(F32)).astype(BF16)


def _split3(x):
    hi = x.astype(BF16)
    r = x - hi.astype(F32)
    mid = r.astype(BF16)
    return hi, mid, (r - mid.astype(F32)).astype(BF16)


def _mm3(a, b):
    cross = _dot(jnp.concatenate([a[0], a[1]], axis=1), jnp.concatenate([b[1], b[0]], axis=0))
    return _dot(a[0], b[0]) + cross


def _mm3_nt(a, b):
    cross = _dot_nt(jnp.concatenate([a[0], a[1]], axis=1), jnp.concatenate([b[1], b[0]], axis=1))
    return _dot_nt(a[0], b[0]) + cross


def _dot_x3(a, b):
    return _mm3(_split2(a), _split2(b))


def _dot_exact_lhs(sel, x):
    hi, mid, lo = _split3(x)
    return _dot(sel, hi) + (_dot(sel, mid) + _dot(sel, lo))


def _dot_exact_rhs(x, sel):
    hi, mid, lo = _split3(x)
    return _dot(hi, sel) + (_dot(mid, sel) + _dot(lo, sel))


def _head_sums(x, blk):
    hi, lo = _split2(x)
    return _dot(hi, blk) + _dot(lo, blk)


def _ln(x):
    mu = jnp.mean(x, axis=-1, keepdims=True)
    xc = x - mu
    var = jnp.mean(xc * xc, axis=-1, keepdims=True)
    return xc * lax.rsqrt(var + LN_EPS)


def _softplus(x):
    return jnp.maximum(x, 0.0) + jnp.log1p(jnp.exp(-jnp.abs(x)))


def _sigmoid(x):
    return 1.0 / (1.0 + jnp.exp(-x))


def _params(sem):
    return pltpu.CompilerParams(dimension_semantics=sem, vmem_limit_bytes=VMEM_LIMIT)


def _pack_rows(rows, width):
    rows = [r.reshape(1, width).astype(F32) for r in rows]
    pad = jnp.zeros((8 - len(rows), width), F32)
    return jnp.concatenate(rows + [pad], axis=0)


def _ada_kernel(c_ref, w_ref, b_ref, o_ref):
    c = c_ref[...]
    ca = c * _sigmoid(c)
    o_ref[0] = _dot(ca, w_ref[0], HI) + b_ref[0]


def _ada_all(c, ada_w, ada_b):
    n_layers, d, n = ada_w.shape
    tn = 2304
    c8 = jnp.broadcast_to(c.astype(F32), (8, d))
    out = pl.pallas_call(
        _ada_kernel,
        grid=(n_layers, n // tn),
        in_specs=[
            pl.BlockSpec((8, d), lambda l, j: (0, 0)),
            pl.BlockSpec((1, d, tn), lambda l, j: (l, 0, j)),
            pl.BlockSpec((1, 1, tn), lambda l, j: (l, 0, j)),
        ],
        out_specs=pl.BlockSpec((1, 8, tn), lambda l, j: (l, 0, j)),
        out_shape=jax.ShapeDtypeStruct((n_layers, 8, n), F32),
        compiler_params=_params(("arbitrary", "arbitrary")),
    )(c8, ada_w, ada_b.reshape(n_layers, 1, n))
    return out[:, 0, :]


def _ffn_kernel(x_ref, mod_ref, lnp_ref, wup_ref, wd_ref, o_ref, *next_ref, alpha, tf):
    d_ff = wd_ref.shape[0]
    n_chunks = d_ff // tf
    x = x_ref[...]
    h = (_ln(x) * (1.0 + mod_ref[1:2, :]) + mod_ref[0:1, :]).astype(BF16)

    def up(c):
        u = _dot(h, wup_ref[:, c * tf:(c + 1) * tf])
        g = _dot(h, wup_ref[:, d_ff + c * tf:d_ff + (c + 1) * tf])
        return u, g

    acc = None
    nxt = up(0)
    for c in range(n_chunks):
        u, g = nxt
        if c + 1 < n_chunks:
            nxt = up(c + 1)
        act = (u * _sigmoid(u) * g).astype(BF16)
        down = _dot(act, wd_ref[c * tf:(c + 1) * tf, :])
        acc = down if acc is None else acc + down

    z = alpha * x + (0.5 * mod_ref[2:3, :]) * acc
    out = _ln(z) * lnp_ref[0:1, :] + lnp_ref[1:2, :]
    o_ref[...] = out
    if next_ref:
        next_ref[0][...] = (_ln(out) * (1.0 + lnp_ref[3:4, :]) + lnp_ref[2:3, :]).astype(BF16)


def _ffn(x, mod, lnp, w_up, w_down, alpha, emit_next):
    s, d = x.shape
    d_ff = w_down.shape[0]
    tm = min(512, s)
    rows = pl.BlockSpec((tm, d), lambda i: (i, 0))
    out_specs, out_shape = rows, jax.ShapeDtypeStruct((s, d), F32)
    if emit_next:
        out_specs, out_shape = [rows, rows], [out_shape, jax.ShapeDtypeStruct((s, d), BF16)]
    return pl.pallas_call(
        functools.partial(_ffn_kernel, alpha=alpha, tf=256),
        grid=(s // tm,),
        in_specs=[
            rows,
            pl.BlockSpec((8, d), lambda i: (0, 0)),
            pl.BlockSpec((8, d), lambda i: (0, 0)),
            pl.BlockSpec((d, 2 * d_ff), lambda i: (0, 0)),
            pl.BlockSpec((d_ff, d), lambda i: (0, 0)),
        ],
        out_specs=out_specs,
        out_shape=out_shape,
        compiler_params=_params(("parallel",)),
    )(x, mod, lnp, w_up, w_down)


def _in_proj_kernel(h_ref, w_ref, o_ref):
    o_ref[...] = _dot(h_ref[...], w_ref[...])


def _in_proj(h, w):
    s, d = h.shape
    n = w.shape[1]
    tm = min(2048, s)
    tn = 1024
    return pl.pallas_call(
        _in_proj_kernel,
        grid=(s // tm, n // tn),
        in_specs=[
            pl.BlockSpec((tm, d), lambda i, j: (i, 0)),
            pl.BlockSpec((d, tn), lambda i, j: (0, j)),
        ],
        out_specs=pl.BlockSpec((tm, tn), lambda i, j: (i, j)),
        out_shape=jax.ShapeDtypeStruct((s, n), F32),
        compiler_params=_params(("parallel", "parallel")),
    )(h, w)


def _fox_pack_kernel(qkv_ref, fl_ref, bias_ref, qp_ref, kp_ref, vp_ref, fs_ref, carry):
    tm = fl_ref.shape[0]

    @pl.when(pl.program_id(0) == 0)
    def _():
        carry[...] = jnp.zeros_like(carry)

    log_f = -_softplus(-(fl_ref[...] + bias_ref[0:1, :]))
    row = lax.broadcasted_iota(jnp.int32, (tm, tm), 0)
    col = lax.broadcasted_iota(jnp.int32, (tm, tm), 1)
    tril = jnp.where(col <= row, 1.0, 0.0).astype(BF16)
    f_loc = _dot_exact_lhs(tril, log_f * LOG2E)

    start = carry[...]
    r8 = lax.broadcasted_iota(jnp.int32, (8, LANES), 0)
    c8 = lax.broadcasted_iota(jnp.int32, (8, LANES), 1)
    diag = jnp.where(r8 == c8, start, 0.0)
    fs_ref[0] = _dot_exact_rhs(diag, jnp.ones((LANES, LANES), BF16))
    carry[...] = start + f_loc[tm - 1:tm, :]

    lane = lax.broadcasted_iota(jnp.int32, (1, LANES), 1)
    head_lanes = lane < HEAD_DIM
    ones3 = jnp.logical_and(lane >= LANE_ONE, lane < LANE_ONE + 3).astype(F32)
    onesf = jnp.logical_and(lane >= LANE_F, lane < LANE_F + 3).astype(F32)
    one1 = (lane == LANE_ONE).astype(F32)
    for h in range(N_HEADS):
        f = jnp.broadcast_to(f_loc[:, h:h + 1], (tm, LANES))
        pieces = [p.astype(F32) for p in _split3(f)]
        q_extra = ones3
        k_extra = onesf
        for n, piece in enumerate(pieces):
            q_extra = q_extra + jnp.where(lane == LANE_F + n, piece, 0.0)
            k_extra = k_extra - jnp.where(lane == LANE_ONE + n, piece, 0.0)
        pair = h // 2
        heads = []
        for base in (0, ATT_WIDTH, 2 * ATT_WIDTH):
            xh = qkv_ref[:, base + pair * LANES:base + (pair + 1) * LANES]
            heads.append(pltpu.roll(xh, HEAD_DIM, axis=1) if h % 2 else xh)
        sl = slice(h * LANES, (h + 1) * LANES)
        qp_ref[:, sl] = jnp.where(head_lanes, heads[0] * (LOG2E * HEAD_DIM ** -0.5), q_extra).astype(BF16)
        kp_ref[:, sl] = jnp.where(head_lanes, heads[1], k_extra).astype(BF16)
        vp_ref[:, sl] = jnp.where(head_lanes, heads[2], one1).astype(BF16)


def _fox_pack(z_all, bias):
    s = z_all.shape[0]
    tm = min(ATT_TILE, s)
    wide = N_HEADS * LANES
    packed = jax.ShapeDtypeStruct((s, wide), BF16)
    return pl.pallas_call(
        _fox_pack_kernel,
        grid=(s // tm,),
        in_specs=[
            pl.BlockSpec((tm, 3 * ATT_WIDTH), lambda i: (i, COL_QKV // (3 * ATT_WIDTH))),
            pl.BlockSpec((tm, LANES), lambda i: (i, COL_FL // LANES)),
            pl.BlockSpec((8, LANES), lambda i: (0, 0)),
        ],
        out_specs=[pl.BlockSpec((tm, wide), lambda i: (i, 0))] * 3
        + [pl.BlockSpec((1, 8, LANES), lambda i: (i, 0, 0))],
        out_shape=[packed] * 3 + [jax.ShapeDtypeStruct((s // tm, 8, LANES), F32)],
        scratch_shapes=[pltpu.VMEM((8, LANES), F32)],
        compiler_params=_params(("arbitrary",)),
    )(z_all, z_all, bias)


def _attn_kernel(q_ref, k_ref, v_ref, fs_ref, o_ref, m_scr, acc_scr, *, tile):
    h = pl.program_id(0)
    i = pl.program_id(1)
    q = q_ref[...]
    reps = tile // LANES
    row = lax.broadcasted_iota(jnp.int32, (tile, tile), 0)
    col = lax.broadcasted_iota(jnp.int32, (tile, tile), 1)
    fs_q = fs_ref[i, pl.ds(h, 1), :]

    m_scr[...] = jnp.full(m_scr.shape, NEG_BIG, F32)
    acc_scr[...] = jnp.zeros_like(acc_scr)

    def run(tiles):
        offs = [pl.multiple_of(j * tile, tile) for j, _, _ in tiles]
        m_old = [m_scr[slot] for _, slot, _ in tiles]
        acc_old = [acc_scr[slot] for _, slot, _ in tiles]
        shift = [fs_q - fs_ref[j, pl.ds(h, 1), :] for j, _, _ in tiles]
        s = [_dot_nt(q, k_ref[pl.ds(off, tile), :]) for off in offs]
        s = [jnp.where(col <= row, x, NEG_BIG) if masked else x for x, (_, _, masked) in zip(s, tiles)]
        part = []
        for x in s:
            pm = x[:, 0:LANES]
            for n in range(1, reps):
                pm = jnp.maximum(pm, x[:, n * LANES:(n + 1) * LANES])
            part.append(pm)
        m_new = [jnp.maximum(mo, jnp.max(pm, axis=-1, keepdims=True) + sh)
                 for mo, pm, sh in zip(m_old, part, shift)]
        p = [jnp.exp2(x - pltpu.repeat(mn - sh, reps, axis=1)).astype(BF16) for x, mn, sh in zip(s, m_new, shift)]
        pv = [_dot(x, v_ref[pl.ds(off, tile), :]) for x, off in zip(p, offs)]
        for n, (_, slot, _) in enumerate(tiles):
            acc_scr[slot] = jnp.exp2(m_old[n] - m_new[n]) * acc_old[n] + pv[n]
            m_scr[slot] = m_new[n]

    def body(jj, carry):
        run([(ATT_SLOTS * jj + n, n, False) for n in range(ATT_SLOTS)])
        return carry

    groups = i // ATT_SLOTS
    lax.fori_loop(0, groups, body, 0)

    for rem in range(ATT_SLOTS):
        @pl.when(i % ATT_SLOTS == rem)
        def _(rem=rem):
            run([(ATT_SLOTS * groups + n, n, False) for n in range(rem)] + [(i, rem, True)])

    m = m_scr[0]
    for n in range(1, ATT_SLOTS):
        m = jnp.maximum(m, m_scr[n])
    acc = jnp.exp2(m_scr[0] - m) * acc_scr[0]
    for n in range(1, ATT_SLOTS):
        acc = acc + jnp.exp2(m_scr[n] - m) * acc_scr[n]
    lane = lax.broadcasted_iota(jnp.int32, (1, LANES), 1)
    o = jnp.where(lane < HEAD_DIM, acc / acc[:, LANE_ONE:LANE_ONE + 1], 0.0)
    o_ref[...] = o.astype(o_ref.dtype)


def _attention(qp, kp, vp, fs):
    s, wide = qp.shape
    tile = min(ATT_TILE, s)
    return pl.pallas_call(
        functools.partial(_attn_kernel, tile=tile),
        grid=(N_HEADS, s // tile),
        in_specs=[
            pl.BlockSpec((tile, LANES), lambda h, i: (i, h)),
            pl.BlockSpec((s, LANES), lambda h, i: (0, h)),
            pl.BlockSpec((s, LANES), lambda h, i: (0, h)),
            pl.BlockSpec(fs.shape, lambda h, i: (0, 0, 0)),
        ],
        out_specs=pl.BlockSpec((tile, LANES), lambda h, i: (i, h)),
        out_shape=jax.ShapeDtypeStruct((s, wide), BF16),
        scratch_shapes=[pltpu.VMEM((ATT_SLOTS, tile, LANES), F32), pltpu.VMEM((ATT_SLOTS, tile, LANES), F32)],
        compiler_params=_params(("parallel", "arbitrary")),
    )(qp, kp, vp, fs)


def _lru_kernel(xb_ref, yb_ref, par_ref, cw_ref, ga_ref, gx_ref, o_ref,
                xpad_scr, a_scr, u_scr, h_scr, carry_scr):
    tm, w = xb_ref.shape

    @pl.when(pl.program_id(0) == 0)
    def _():
        xpad_scr[0:8, :] = jnp.zeros((8, w), F32)
        carry_scr[...] = jnp.zeros_like(carry_scr)

    xb = xb_ref[...]
    xpad_scr[8:8 + tm, :] = xb
    xc = xb * cw_ref[LRU_CONV - 1:LRU_CONV, :] + par_ref[0:1, :]
    for d in range(1, LRU_CONV):
        xc = xc + xpad_scr[8 - d:8 - d + tm, :] * cw_ref[LRU_CONV - 1 - d:LRU_CONV - d, :]
    xpad_scr[0:8, :] = xb[tm - 8:tm, :]

    xcb = xc.astype(BF16)
    r = _sigmoid(_dot(xcb, ga_ref[...]) + par_ref[1:2, :])
    gi = _sigmoid(_dot(xcb, gx_ref[...]) + par_ref[2:3, :])
    log_a = (-LRU_C) * r * _softplus(-par_ref[3:4, :])
    a = jnp.exp(log_a)
    u = jnp.sqrt(-jnp.tanh(log_a) * (a * a + 1.0)) * (gi * xc)

    r8 = lax.broadcasted_iota(jnp.int32, (tm, w), 0) % 8
    for d in (1, 2, 4):
        keep = r8 >= d
        a_prev = jnp.where(keep, pltpu.roll(a, d, axis=0), 1.0)
        u_prev = jnp.where(keep, pltpu.roll(u, d, axis=0), 0.0)
        u = a * u_prev + u
        a = a * a_prev
    a_scr[...] = a
    u_scr[...] = u

    def slab(k, carry):
        off = pl.multiple_of(k * 8, 8)
        h8 = a_scr[pl.ds(off, 8), :] * carry + u_scr[pl.ds(off, 8), :]
        h_scr[pl.ds(off, 8), :] = h8
        return jnp.broadcast_to(h8[7:8, :], (8, w))

    carry_scr[...] = lax.fori_loop(0, tm // 8, slab, carry_scr[...], unroll=4)

    yb = yb_ref[...]
    gelu = 0.5 * yb * (1.0 + jnp.tanh(0.7978845608028654 * (yb + 0.044715 * (yb * yb * yb))))
    o_ref[...] = (h_scr[...] * gelu).astype(o_ref.dtype)


def _lru(z_lru, par, conv_w, ga, gx):
    s = z_lru.shape[0]
    w = LRU_WIDTH
    tm = min(512, s)
    return pl.pallas_call(
        _lru_kernel,
        grid=(s // tm,),
        in_specs=[
            pl.BlockSpec((tm, w), lambda i: (i, COL_LRU // w)),
            pl.BlockSpec((tm, w), lambda i: (i, COL_LRU // w + 1)),
            pl.BlockSpec((8, w), lambda i: (0, 0)),
            pl.BlockSpec((8, w), lambda i: (0, 0)),
            pl.BlockSpec((w, w), lambda i: (0, 0)),
            pl.BlockSpec((w, w), lambda i: (0, 0)),
        ],
        out_specs=pl.BlockSpec((tm, w), lambda i: (i, 0)),
        out_shape=jax.ShapeDtypeStruct((s, w), BF16),
        scratch_shapes=[
            pltpu.VMEM((tm + 8, w), F32),
            pltpu.VMEM((tm, w), F32),
            pltpu.VMEM((tm, w), F32),
            pltpu.VMEM((tm, w), F32),
            pltpu.VMEM((8, w), F32),
        ],
        compiler_params=_params(("arbitrary",)),
    )(z_lru, z_lru, par, conv_w, ga, gx)


def _rwkv_prep_kernel(z_ref, mu_ref, par_ref, w2_ref, a2_ref, g2_ref, blk_ref,
                      r_ref, ld_ref, k_ref, v_ref, an_ref, b_ref, bonus_ref, g_ref, prev_scr):
    tm = z_ref.shape[0]
    wd = RWKV_WIDTH

    @pl.when(pl.program_id(0) == 0)
    def _():
        prev_scr[...] = jnp.zeros_like(prev_scr)

    z = z_ref[...]
    row = lax.broadcasted_iota(jnp.int32, z.shape, 0)
    z_prev = jnp.where(row == 0, prev_scr[0:1, :], pltpu.roll(z, 1, axis=0))
    prev_scr[...] = jnp.broadcast_to(z[tm - 1:tm, :], prev_scr.shape)
    zs = z + (z_prev - z) * mu_ref[0:1, :]

    r = zs[:, 0:wd]
    k = zs[:, wd:2 * wd]
    v = zs[:, 2 * wd:3 * wd]
    wl = zs[:, 3 * wd:3 * wd + LORA_PAD]
    al = zs[:, 3 * wd + LORA_PAD:3 * wd + 2 * LORA_PAD]
    gl = zs[:, 3 * wd + 2 * LORA_PAD:3 * wd + 3 * LORA_PAD]
    w0, a0, k_k, k_a, r_k = (par_ref[n:n + 1, :] for n in range(5))

    w = -_softplus(-(w0 + _dot_x3(jnp.tanh(wl), w2_ref[...]))) - 0.5
    a = _sigmoid(a0 + _dot_x3(al, a2_ref[...]))
    g = _dot_x3(_sigmoid(gl), g2_ref[...])

    blk = blk_ref[...]
    kk = k * k_k
    norm = jnp.sqrt(_head_sums(kk * kk, blk))
    kk = kk / jnp.maximum(norm, 1e-12)
    kf = k * (1.0 + (a - 1.0) * k_a)

    r_ref[...] = r
    ld_ref[...] = -jnp.exp(w)
    k_ref[...] = kf
    v_ref[...] = v
    an_ref[...] = -kk
    b_ref[...] = kk * a
    bonus_ref[...] = _head_sums(r * kf * r_k, blk) * v
    g_ref[...] = g


def _rwkv_prep(zc, mu, par, w2, a2, g2, blk):
    s = zc.shape[0]
    n = RWKV_COLS_PAD
    wd = RWKV_WIDTH
    tm = min(512, s)
    full = lambda shape: pl.BlockSpec(shape, lambda i: (0, 0))
    out = jax.ShapeDtypeStruct((s, wd), F32)
    return pl.pallas_call(
        _rwkv_prep_kernel,
        grid=(s // tm,),
        in_specs=[
            pl.BlockSpec((tm, n), lambda i: (i, COL_RWKV // n)),
            full((8, n)), full((8, wd)), full((LORA_PAD, wd)), full((LORA_PAD, wd)),
            full((LORA_PAD, wd)), full((wd, wd)),
        ],
        out_specs=[pl.BlockSpec((tm, wd), lambda i: (i, 0))] * 8,
        out_shape=[out] * 8,
        scratch_shapes=[pltpu.VMEM((8, n), F32)],
        compiler_params=_params(("arbitrary",)),
    )(zc, mu, par, w2, a2, g2, blk)


def _rwkv_scan_kernel(r_ref, ld_ref, k_ref, v_ref, an_ref, b_ref, y_ref, state_scr):
    t = RWKV_CHUNK
    n_chunks = r_ref.shape[0] // t

    @pl.when(pl.program_id(0) == 0)
    def _():
        state_scr[...] = jnp.zeros_like(state_scr)

    gh = SCAN_GROUP_HEADS
    gw = gh * HEAD_DIM
    tg = gh * t
    groups = range(N_HEADS // gh)
    chunks = range(n_chunks)
    units = [(c, g) for c in chunks for g in groups]
    sls = [slice(g * gw, (g + 1) * gw) for g in groups]
    cat0 = lambda xs: jnp.concatenate(xs, axis=0)

    lane = lax.broadcasted_iota(jnp.int32, (1, gw), 1)
    head_masks = [jnp.logical_and(lane >= n * HEAD_DIM, lane < (n + 1) * HEAD_DIM) for n in range(gh)]
    rowg = lax.broadcasted_iota(jnp.int32, (tg, tg), 0)
    colg = lax.broadcasted_iota(jnp.int32, (tg, tg), 1)
    same_head = (rowg // t) == (colg // t)
    strict = jnp.logical_and(same_head, (colg % t) < (rowg % t))
    incl = jnp.logical_and(same_head, (colg % t) <= (rowg % t))

    def stack(x):
        return jnp.concatenate([jnp.where(mask, x, 0.0) for mask in head_masks], axis=0)

    row3 = lax.broadcasted_iota(jnp.int32, (t, 3 * t), 0)
    col3 = lax.broadcasted_iota(jnp.int32, (t, 3 * t), 1)
    tril3 = jnp.where(col3 % t <= row3, 1.0, 0.0).astype(BF16)
    rows = [slice(c * t, (c + 1) * t) for c in chunks]
    ld = [ld_ref[rs, :] for rs in rows]
    c_incl = [_dot(tril3, jnp.concatenate(_split3(x), axis=0)) for x in ld]
    p_incl = [jnp.exp(x) for x in c_incl]
    p_inv = [jnp.exp(-x) for x in c_incl]
    r_s = [r_ref[rows[c], :] * p_incl[c] for c in chunks]
    a_s = [an_ref[rows[c], :] * jnp.exp(c_incl[c] - ld[c]) for c in chunks]
    b_s = [b_ref[rows[c], :] * p_inv[c] for c in chunks]
    k_s = [k_ref[rows[c], :] * p_inv[c] for c in chunks]

    ar = {(c, g): cat0([stack(a_s[c][:, sls[g]]), stack(r_s[c][:, sls[g]])]).astype(BF16) for c, g in units}
    v2 = {(c, g): stack(v_ref[rows[c], sls[g]]) for c, g in units}
    bk = {(c, g): cat0([stack(b_s[c][:, sls[g]]), stack(k_s[c][:, sls[g]])]).astype(BF16) for c, g in units}
    m = {u: _dot_nt(ar[u], bk[u]) for u in units}
    lt_k = {u: cat0([jnp.where(strict, m[u][0:tg, tg:2 * tg], 0.0),
                     jnp.where(incl, m[u][tg:2 * tg, tg:2 * tg], 0.0)]).astype(BF16) for u in units}
    from_v = {u: _dot(lt_k[u], v2[u].astype(BF16)) for u in units}
    t_rb = {u: jnp.where(incl, m[u][tg:2 * tg, 0:tg], 0.0).astype(BF16) for u in units}
    l_all = {u: jnp.where(strict, m[u][0:tg, 0:tg], 0.0) for u in units}

    base = SCAN_BASE_BLOCK
    in_base = (rowg // base) == (colg // base)
    eye = jnp.where(rowg == colg, 1.0, 0.0).astype(F32)
    power = {u: jnp.where(in_base, l_all[u], 0.0) for u in units}
    inv = {u: eye + power[u] for u in units}
    for _ in range(max(1, (base - 1).bit_length()) - 1):
        power_s = {u: _split2(power[u]) for u in units}
        power = {u: _mm3(power_s[u], power_s[u]) for u in units}
        inv = {u: inv[u] + _mm3(_split2(power[u]), _split2(inv[u])) for u in units}
    blk_size = base
    while blk_size < t:
        lower_left = jnp.logical_and(
            jnp.logical_and((rowg // (2 * blk_size)) == (colg // (2 * blk_size)), (rowg % (2 * blk_size)) >= blk_size),
            (colg % (2 * blk_size)) < blk_size)
        inv_s = {u: _split2(inv[u]) for u in units}
        c_a = {u: _mm3(_split2(jnp.where(lower_left, l_all[u], 0.0)), inv_s[u]) for u in units}
        inv = {u: inv[u] + _mm3(inv_s[u], _split2(c_a[u])) for u in units}
        blk_size *= 2
    inv_s = {u: _split2(inv[u]) for u in units}

    states = [state_scr[g] for g in groups]
    for c in chunks:
        from_state = [_dot_nt(ar[c, g], states[g].astype(BF16)) for g in groups]
        sol = [_mm3(inv_s[c, g], _split2(from_state[g][0:tg] + from_v[c, g][0:tg])) for g in groups]
        for g in groups:
            y2 = from_state[g][tg:2 * tg] + from_v[c, g][tg:2 * tg] + _dot(t_rb[c, g], sol[g].astype(BF16))
            y = y2[0:t, :]
            for n in range(1, gh):
                y = y + y2[n * t:(n + 1) * t, :]
            y_ref[rows[c], sls[g]] = y
        upd = [_dot(cat0([sol[g], v2[c, g]]).T.astype(BF16), bk[c, g]) for g in groups]
        states = [(states[g] + upd[g]) * p_incl[c][t - 1:t, sls[g]] for g in groups]
    for g in groups:
        state_scr[g] = states[g]


def _rwkv_scan(r, ld, k, v, an, b):
    s, wd = r.shape
    rows = min(RWKV_CHUNK * SCAN_CHUNKS_PER_STEP, s)
    spec = pl.BlockSpec((rows, wd), lambda i: (i, 0))
    return pl.pallas_call(
        _rwkv_scan_kernel,
        grid=(s // rows,),
        in_specs=[spec] * 6,
        out_specs=spec,
        out_shape=jax.ShapeDtypeStruct((s, wd), F32),
        scratch_shapes=[pltpu.VMEM((N_HEADS // SCAN_GROUP_HEADS,) + (SCAN_GROUP_HEADS * HEAD_DIM,) * 2, F32)],
        compiler_params=_params(("arbitrary",)),
    )(r, ld, k, v, an, b)


def _merge_kernel(x_ref, oa_ref, ob_ref, y_ref, bonus_ref, g_ref, ga_ref, gb_ref, gc_ref,
                  par_ref, gn_ref, blk_ref, wpa_ref, wpb_ref, wpc_ref, wo_ref, o_ref, *, alpha):
    blk = blk_ref[...]
    y = y_ref[...]
    mean = _dot_exact_rhs(y, blk) * (1.0 / HEAD_DIM)
    yc = y - mean
    var = _head_sums(yc * yc, blk) * (1.0 / HEAD_DIM)
    yn = yc * lax.rsqrt(var + RWKV_GN_EPS) * gn_ref[0:1, :] + gn_ref[1:2, :]
    oc = ((yn + bonus_ref[...]) * g_ref[...]).astype(BF16)

    merged = (_sigmoid(ga_ref[...]) * _dot(oa_ref[...], wpa_ref[...])
              + _sigmoid(gb_ref[...]) * _dot(ob_ref[...], wpb_ref[...])
              + _sigmoid(gc_ref[...]) * _dot(oc, wpc_ref[...]))
    out = _dot(merged.astype(BF16), wo_ref[...])
    z = alpha * x_ref[...] + par_ref[0:1, :] * out
    o_ref[...] = _ln(z) * par_ref[1:2, :] + par_ref[2:3, :]


def _merge(x, oa, ob, y, bonus, g, z_all, par, gn, blk, wpa, wpb, wpc, wo, alpha):
    s, d = x.shape
    wd = RWKV_WIDTH
    tm = min(512, s)
    rows = lambda width, col=0: pl.BlockSpec((tm, width), lambda i, col=col: (i, col))
    full = lambda shape: pl.BlockSpec(shape, lambda i: (0, 0))
    gate0 = COL_GATE // d
    return pl.pallas_call(
        functools.partial(_merge_kernel, alpha=alpha),
        grid=(s // tm,),
        in_specs=[
            rows(d), rows(oa.shape[1]), rows(LRU_WIDTH), rows(wd), rows(wd), rows(wd),
            rows(d, gate0), rows(d, gate0 + 1), rows(d, gate0 + 2),
            full((8, d)), full((8, wd)), full((wd, wd)),
            full(wpa.shape), full((LRU_WIDTH, d)), full((wd, d)), full((d, d)),
        ],
        out_specs=rows(d),
        out_shape=jax.ShapeDtypeStruct((s, d), F32),
        compiler_params=_params(("parallel",)),
    )(x, oa, ob, y, bonus, g, z_all, z_all, z_all, par, gn, blk, wpa, wpb, wpc, wo)


def _block_diag(w):
    nb, n, _ = w.shape
    eye = jnp.eye(nb, dtype=w.dtype)
    return (eye[:, None, :, None] * w[:, :, None, :]).reshape(nb * n, nb * n)


def kernel(x, c, ada_w, ada_b, ln_g, ln_b, ffn_up, ffn_down, w_in, fox_f_bias, lru_conv_w, lru_conv_b, lru_ga_w, lru_ga_b, lru_gx_w, lru_gx_b, lru_lambda, rwkv_mu, rwkv_w0, rwkv_w2, rwkv_a0, rwkv_a2, rwkv_g2, rwkv_k_k, rwkv_k_a, rwkv_r_k, rwkv_gn_w, rwkv_gn_b, w_proj_a, w_proj_b, w_proj_c, w_out):
    batch, s, d = x.shape
    assert batch == 1 and d == 1024
    depth = ada_w.shape[0]
    alpha = float((2 * depth) ** 0.25)
    wd = RWKV_WIDTH

    ada = _ada_all(c, ada_w, ada_b)
    blk = _block_diag(jnp.ones((N_HEADS, HEAD_DIM, HEAD_DIM), BF16))

    o_att = 3 * ATT_WIDTH
    o_lru = o_att + N_HEADS
    o_rwkv = o_lru + 2 * LRU_WIDTH
    o_gate = o_rwkv + 3 * wd + 64 + 64 + 128
    lora = 64

    xs = x[0]
    for l in range(depth):
        mods = [ada[l, n * d:(n + 1) * d] for n in range(9)]
        w_l = w_in[l]

        xs, h_mix = _ffn(xs, _pack_rows(mods[0:3], d),
                         _pack_rows([ln_g[l, 0], ln_b[l, 0], mods[3], mods[4]], d),
                         ffn_up[l, 0].astype(BF16), ffn_down[l, 0].astype(BF16), alpha, True)

        w_c = w_l[:, o_rwkv:o_gate]
        zpad = lambda n: jnp.zeros((d, n), F32)
        w_all = jnp.concatenate([
            w_l[:, 0:o_att],
            w_l[:, o_att:o_lru], zpad(COL_LRU - COL_FL - N_HEADS),
            w_l[:, o_lru:o_rwkv],
            w_c[:, 0:3 * wd],
            w_c[:, 3 * wd:3 * wd + lora], zpad(LORA_PAD - lora),
            w_c[:, 3 * wd + lora:3 * wd + 2 * lora], zpad(LORA_PAD - lora),
            w_c[:, 3 * wd + 2 * lora:], zpad(RWKV_COLS_PAD - 3 * wd - 3 * LORA_PAD),
            w_l[:, o_gate:]], axis=1).astype(BF16)
        assert w_all.shape[1] == N_IN_PAD
        z_all = _in_proj(h_mix, w_all)

        f_bias = _pack_rows([jnp.pad(fox_f_bias[l], (0, LANES - N_HEADS))], LANES)
        qp, kp, vp, fs = _fox_pack(z_all, f_bias)
        o_a = _attention(qp, kp, vp, fs)
        wpa = jnp.pad(w_proj_a[l].reshape(N_HEADS, HEAD_DIM, d), ((0, 0), (0, LANES - HEAD_DIM), (0, 0)))
        wpa = wpa.reshape(N_HEADS * LANES, d).astype(BF16)

        lru_par = _pack_rows([lru_conv_b[l], lru_ga_b[l], lru_gx_b[l], lru_lambda[l]], LRU_WIDTH)
        conv_w = jnp.pad(lru_conv_w[l], ((0, 8 - LRU_CONV), (0, 0)))
        o_b = _lru(z_all, lru_par, conv_w, _block_diag(lru_ga_w[l]).astype(BF16),
                   _block_diag(lru_gx_w[l]).astype(BF16))

        mu = rwkv_mu[l]
        zv = lambda n: jnp.zeros((n,), F32)
        mu_pad = jnp.concatenate([
            mu[0:3 * wd], mu[3 * wd:3 * wd + lora], zv(LORA_PAD - lora),
            mu[3 * wd + lora:3 * wd + 2 * lora], zv(LORA_PAD - lora),
            mu[3 * wd + 2 * lora:], zv(RWKV_COLS_PAD - 3 * wd - 3 * LORA_PAD)])
        rw_par = _pack_rows([rwkv_w0[l], rwkv_a0[l], rwkv_k_k[l], rwkv_k_a[l], rwkv_r_k[l].reshape(wd)], wd)
        pad_rows = lambda w: jnp.pad(w, ((0, LORA_PAD - w.shape[0]), (0, 0)))
        r, ld, kf, v, an, b, bonus, g = _rwkv_prep(
            z_all, _pack_rows([mu_pad], RWKV_COLS_PAD), rw_par,
            pad_rows(rwkv_w2[l]), pad_rows(rwkv_a2[l]), rwkv_g2[l], blk)
        y = _rwkv_scan(r, ld, kf, v, an, b)

        par = _pack_rows([mods[5], ln_g[l, 1], ln_b[l, 1]], d)
        gn = _pack_rows([rwkv_gn_w[l], rwkv_gn_b[l]], wd)
        xs = _merge(xs, o_a, o_b, y, bonus, g, z_all, par, gn, blk,
                    wpa, w_proj_b[l].astype(BF16), w_proj_c[l].astype(BF16),
                    w_out[l].astype(BF16), alpha)

        xs = _ffn(xs, _pack_rows(mods[6:9], d), _pack_rows([ln_g[l, 2], ln_b[l, 2]], d),
                  ffn_up[l, 1].astype(BF16), ffn_down[l, 1].astype(BF16), alpha, False)

    return xs[None]
```
